```python
import math
import jax, jax.numpy as jnp
from jax import lax
import numpy as np

D_MODEL = 1024
BATCH = 32
SEQ = 256
DEPTH = 2
DEC_BATCH = 8
DEC_SEQ = 1024
PAST_LEN = 256

GRID_W = 64
ROPE_BASE = 10000.0
EPS = 1e-6
Q_BLOCK = 128
MLA_HEADS = 8
MLA_NOPE = 64
MLA_ROPE = 32
MLA_V = 64
MLA_Q_RANK = 384
MLA_KV_RANK = 256
GQA_HEADS = 8
GQA_KV_HEADS = 2
GQA_DIM = 64
DIFF_HEADS = 4
DIFF_DIM = 64
DIFF_V = 2 * DIFF_DIM
N_EXPERTS = 16
EXPERT_FF = 1024
CAPACITY_FACTOR = 2
N_BRANCH = 3
IN_SIZES = (MLA_Q_RANK, MLA_KV_RANK, MLA_ROPE,
            GQA_HEADS * GQA_DIM, GQA_KV_HEADS * GQA_DIM, GQA_KV_HEADS * GQA_DIM,
            DIFF_HEADS * 2 * DIFF_DIM, DIFF_HEADS * 2 * DIFF_DIM, DIFF_HEADS * DIFF_V,
            D_MODEL, D_MODEL, D_MODEL)
IN_COLS = sum(IN_SIZES)

kernel_name = 'hybrid_diffusion_mla_gqa_diffattn_ec_step'


def rmsnorm(x, g):
    xf = x.astype(jnp.float32)
    y = xf * lax.rsqrt(jnp.mean(xf * xf, axis=-1, keepdims=True) + EPS)
    return (y * g.astype(jnp.float32)).astype(x.dtype)


def rope_1d(x, pos):
    half = x.shape[-1] // 2
    freqs = ROPE_BASE ** (-jnp.arange(half, dtype=jnp.float32) / half)
    ang = pos[:, None] * freqs[None, :]
    cos = jnp.cos(ang)[:, None, :].astype(x.dtype)
    sin = jnp.sin(ang)[:, None, :].astype(x.dtype)
    x1, x2 = x[..., :half], x[..., half:]
    return jnp.concatenate([x1 * cos - x2 * sin, x1 * sin + x2 * cos], axis=-1)


def rope_2d(x, row, col):
    half = x.shape[-1] // 2
    return jnp.concatenate([rope_1d(x[..., :half], row), rope_1d(x[..., half:], col)], axis=-1)


def attend(q, k, v):
    b, tq, h, dk = q.shape
    g = k.shape[2]
    rep = h // g
    dv = v.shape[-1]
    scale = dk ** -0.5
    qb = q.reshape(b, tq // Q_BLOCK, Q_BLOCK, g, rep, dk).transpose(1, 0, 2, 3, 4, 5)

    def one_block(qblk):
        s = jnp.einsum('bqgrd,bkgd->bgrqk', qblk, k, preferred_element_type=jnp.float32) * scale
        p = jax.nn.softmax(s, axis=-1).astype(v.dtype)
        return jnp.einsum('bgrqk,bkge->bqgre', p, v)

    o = lax.map(one_block, qb)
    return o.transpose(1, 0, 2, 3, 4, 5).reshape(b, tq, h * dv)


def attention_branches(h, p, lam_init, pos, ctx):
    b, t, _ = h.shape
    z = h @ p['w_in']
    splits = [int(s) for s in np.cumsum(IN_SIZES)[:-1]]
    (c_q, c_kv, k_pe, gq, gk, gv, dq, dk, dv, g_mla, g_gqa, g_diff) = jnp.split(z, splits, axis=-1)
    c_kv = rmsnorm(c_kv, p['mla_kv_norm'])
    k_pe = k_pe[:, :, None, :]
    q = (rmsnorm(c_q, p['mla_q_norm']) @ p['mla_w_uq']).reshape(b, t, MLA_HEADS, MLA_NOPE + MLA_ROPE)
    gq = rmsnorm(gq.reshape(b, t, GQA_HEADS, GQA_DIM), p['gqa_q_norm'])
    gk = rmsnorm(gk.reshape(b, t, GQA_KV_HEADS, GQA_DIM), p['gqa_k_norm'])
    gv = gv.reshape(b, t, GQA_KV_HEADS, GQA_DIM)
    dq = dq.reshape(b, t, DIFF_HEADS, 2 * DIFF_DIM)
    dk = dk.reshape(b, t, DIFF_HEADS, 2 * DIFF_DIM)
    dv = dv.reshape(b, t, DIFF_HEADS, DIFF_V)
    own = (c_kv, k_pe[:, :, 0, :], gk, gv, dk, dv)

    if pos is None:
        ckv_all, kpe_all, gk_all, gv_all, dk_all, dv_all = c_kv, k_pe, gk, gv, dk, dv
    else:
        row, col = pos

        def rot(u):
            return rope_2d(u, row, col)

        def rot_pair(u):
            return jnp.concatenate([rot(u[..., :DIFF_DIM]), rot(u[..., DIFF_DIM:])], axis=-1)

        def cat(a, u):
            return jnp.concatenate([a, u], axis=1)

        q = jnp.concatenate([q[..., :MLA_NOPE], rot(q[..., MLA_NOPE:])], axis=-1)
        gq = rot(gq)
        dq = rot_pair(dq)
        ctx_ckv, ctx_kpe, ctx_gk, ctx_gv, ctx_dk, ctx_dv = ctx
        ckv_all = cat(ctx_ckv, c_kv)
        kpe_all = cat(ctx_kpe[:, :, None, :], rot(k_pe))
        gk_all = cat(ctx_gk, rot(gk))
        gv_all = cat(ctx_gv, gv)
        dk_all = cat(ctx_dk, rot_pair(dk))
        dv_all = cat(ctx_dv, dv)

    k_nope = jnp.einsum('btr,rhd->bthd', ckv_all, p['mla_w_uk'])
    k_mla = jnp.concatenate([k_nope, jnp.broadcast_to(kpe_all, k_nope.shape[:-1] + (MLA_ROPE,))], axis=-1)
    v_mla = jnp.einsum('btr,rhd->bthd', ckv_all, p['mla_w_uv'])
    o_mla = attend(q, k_mla, v_mla)
    o_gqa = attend(gq, gk_all, gv_all)
    lam = (jnp.exp(jnp.sum(p['diff_lam_q1'].astype(jnp.float32) * p['diff_lam_k1'].astype(jnp.float32)))
           - jnp.exp(jnp.sum(p['diff_lam_q2'].astype(jnp.float32) * p['diff_lam_k2'].astype(jnp.float32)))
           + lam_init).astype(h.dtype)
    a1 = attend(dq[..., :DIFF_DIM], dk_all[..., :DIFF_DIM], dv_all)
    a2 = attend(dq[..., DIFF_DIM:], dk_all[..., DIFF_DIM:], dv_all)
    o_diff = rmsnorm((a1 - lam * a2).reshape(b, t, DIFF_HEADS, DIFF_V), p['diff_subln']) * (1.0 - lam_init)
    o_diff = o_diff.reshape(b, t, DIFF_HEADS * DIFF_V)
    merged = (jax.nn.sigmoid(g_mla) * (o_mla @ p['w_br_mla'])
              + jax.nn.sigmoid(g_gqa) * (o_gqa @ p['w_br_gqa'])
              + jax.nn.sigmoid(g_diff) * (o_diff @ p['w_br_diff']))
    return merged @ p['w_out'], own


def expert_choice_ffn(h, p):
    b, t, d = h.shape
    n = b * t
    cap = CAPACITY_FACTOR * n // N_EXPERTS
    xs = h.reshape(n, d)
    aff = jax.nn.softmax((xs @ p['w_router']).astype(jnp.float32), axis=-1)
    gate, idx = lax.top_k(aff.T, cap)
    xe = xs[idx]
    hg = jnp.einsum('ecd,edf->ecf', xe, p['w_gate'])
    hu = jnp.einsum('ecd,edf->ecf', xe, p['w_up'])
    ye = jnp.einsum('ecf,efd->ecd', jax.nn.silu(hg) * hu, p['w_down']) * gate[..., None].astype(h.dtype)
    out = jnp.zeros_like(xs).at[idx.reshape(-1)].add(ye.reshape(-1, d))
    return out.reshape(b, t, d)


def trunk_layer(x, cond, p, lam_init, pos, ctx):
    mod = (jax.nn.silu(cond) @ p['w_ada'] + p['b_ada']).reshape(-1, 1, 6 * D_MODEL)
    shift1, scale1, gate1, shift2, scale2, gate2 = jnp.split(mod, 6, axis=-1)
    h = rmsnorm(x, p['norm1']) * (1.0 + scale1) + shift1
    y, own = attention_branches(h, p, lam_init, pos, ctx)
    x = x + gate1 * y
    h = rmsnorm(x, p['norm2']) * (1.0 + scale2) + shift2
    x = x + gate2 * expert_choice_ffn(h, p)
    return x, own


def setup_inputs(seed: int = 0) -> dict:
    key = jax.random.key(seed)
    ks = iter(jax.random.split(key, 48))

    def nrm(shape, scale):
        return jax.random.normal(next(ks), shape, jnp.float32) * scale

    def gain(shape):
        return 1.0 + nrm(shape, 0.05)

    D = D_MODEL
    return {
        'x_prompt': nrm((BATCH, SEQ, D), 1.0),
        'x_sample': nrm((DEC_BATCH, DEC_SEQ, D), 1.0),
        'cache_mla_ckv': nrm((DEC_BATCH, DEPTH, PAST_LEN, MLA_KV_RANK), 1.0),
        'cache_mla_kpe': nrm((DEC_BATCH, DEPTH, PAST_LEN, MLA_ROPE), 1.0),
        'cache_gqa_k': nrm((DEC_BATCH, DEPTH, PAST_LEN, GQA_KV_HEADS, GQA_DIM), 1.0),
        'cache_gqa_v': nrm((DEC_BATCH, DEPTH, PAST_LEN, GQA_KV_HEADS, GQA_DIM), 1.0),
        'cache_diff_k': nrm((DEC_BATCH, DEPTH, PAST_LEN, DIFF_HEADS, 2 * DIFF_DIM), 1.0),
        'cache_diff_v': nrm((DEC_BATCH, DEPTH, PAST_LEN, DIFF_HEADS, DIFF_V), 1.0),
        'c': nrm((DEC_BATCH, D), 1.0),
        'c_ctx': nrm((D,), 1.0),
        'w_ada': nrm((DEPTH, D, 6 * D), 0.5 * D ** -0.5),
        'b_ada': nrm((DEPTH, 6 * D), 0.02),
        'norm1': gain((DEPTH, D)),
        'norm2': gain((DEPTH, D)),
        'w_in': nrm((DEPTH, D, IN_COLS), D ** -0.5),
        'mla_q_norm': gain((DEPTH, MLA_Q_RANK)),
        'mla_kv_norm': gain((DEPTH, MLA_KV_RANK)),
        'mla_w_uq': nrm((DEPTH, MLA_Q_RANK, MLA_HEADS * (MLA_NOPE + MLA_ROPE)), MLA_Q_RANK ** -0.5),
        'mla_w_uk': nrm((DEPTH, MLA_KV_RANK, MLA_HEADS, MLA_NOPE), MLA_KV_RANK ** -0.5),
        'mla_w_uv': nrm((DEPTH, MLA_KV_RANK, MLA_HEADS, MLA_V), MLA_KV_RANK ** -0.5),
        'gqa_q_norm': gain((DEPTH, GQA_DIM)),
        'gqa_k_norm': gain((DEPTH, GQA_DIM)),
        'diff_lam_q1': nrm((DEPTH, DIFF_DIM), 0.1),
        'diff_lam_k1': nrm((DEPTH, DIFF_DIM), 0.1),
        'diff_lam_q2': nrm((DEPTH, DIFF_DIM), 0.1),
        'diff_lam_k2': nrm((DEPTH, DIFF_DIM), 0.1),
        'diff_subln': gain((DEPTH, DIFF_V)),
        'w_br_mla': nrm((DEPTH, MLA_HEADS * MLA_V, D), (MLA_HEADS * MLA_V) ** -0.5),
        'w_br_gqa': nrm((DEPTH, GQA_HEADS * GQA_DIM, D), (GQA_HEADS * GQA_DIM) ** -0.5),
        'w_br_diff': nrm((DEPTH, DIFF_HEADS * DIFF_V, D), (DIFF_HEADS * DIFF_V) ** -0.5),
        'w_out': nrm((DEPTH, D, D), D ** -0.5),
        'w_router': nrm((DEPTH, D, N_EXPERTS), D ** -0.5),
        'w_gate': nrm((DEPTH, N_EXPERTS, D, EXPERT_FF), D ** -0.5),
        'w_up': nrm((DEPTH, N_EXPERTS, D, EXPERT_FF), D ** -0.5),
        'w_down': nrm((DEPTH, N_EXPERTS, EXPERT_FF, D), EXPERT_FF ** -0.5),
        'final_norm': gain((D,)),
    }


def reference(x_prompt, x_sample, cache_mla_ckv, cache_mla_kpe, cache_gqa_k, cache_gqa_v,
              cache_diff_k, cache_diff_v, c, c_ctx, w_ada, b_ada, norm1, norm2, w_in,
              mla_q_norm, mla_kv_norm, mla_w_uq, mla_w_uk, mla_w_uv, gqa_q_norm, gqa_k_norm,
              diff_lam_q1, diff_lam_k1, diff_lam_q2, diff_lam_k2, diff_subln,
              w_br_mla, w_br_gqa, w_br_diff, w_out, w_router, w_gate, w_up, w_down, final_norm):
    rows = x_sample.shape[1] // GRID_W
    row_pos = jnp.repeat(jnp.arange(rows, dtype=jnp.float32), GRID_W)
    col_pos = jnp.tile(jnp.arange(GRID_W, dtype=jnp.float32), rows)
    xp, xs = x_prompt, x_sample
    l_ckv, l_kpe, l_gk, l_gv, l_dk, l_dv = [], [], [], [], [], []
    for l in range(DEPTH):
        p = {
            'w_ada': w_ada[l], 'b_ada': b_ada[l], 'norm1': norm1[l], 'norm2': norm2[l],
            'w_in': w_in[l], 'mla_q_norm': mla_q_norm[l], 'mla_kv_norm': mla_kv_norm[l],
            'mla_w_uq': mla_w_uq[l], 'mla_w_uk': mla_w_uk[l], 'mla_w_uv': mla_w_uv[l],
            'gqa_q_norm': gqa_q_norm[l], 'gqa_k_norm': gqa_k_norm[l],
            'diff_lam_q1': diff_lam_q1[l], 'diff_lam_k1': diff_lam_k1[l],
            'diff_lam_q2': diff_lam_q2[l], 'diff_lam_k2': diff_lam_k2[l], 'diff_subln': diff_subln[l],
            'w_br_mla': w_br_mla[l], 'w_br_gqa': w_br_gqa[l], 'w_br_diff': w_br_diff[l],
            'w_out': w_out[l], 'w_router': w_router[l],
            'w_gate': w_gate[l], 'w_up': w_up[l], 'w_down': w_down[l],
        }
        lam_init = 0.8 - 0.6 * math.exp(-0.3 * l)
        xp, own = trunk_layer(xp, c_ctx, p, lam_init, None, None)
        l_ckv.append(own[0]); l_kpe.append(own[1]); l_gk.append(own[2])
        l_gv.append(own[3]); l_dk.append(own[4]); l_dv.append(own[5])
        ctx = (cache_mla_ckv[:, l], cache_mla_kpe[:, l], cache_gqa_k[:, l],
               cache_gqa_v[:, l], cache_diff_k[:, l], cache_diff_v[:, l])
        xs, _ = trunk_layer(xs, c, p, lam_init, (row_pos, col_pos), ctx)
    y_prompt = rmsnorm(xp, final_norm)
    y_sample = rmsnorm(xs, final_norm)
    new_mla_ckv = jnp.stack(l_ckv, axis=1)
    new_mla_kpe = jnp.stack(l_kpe, axis=1)
    new_gqa_k = jnp.stack(l_gk, axis=1)
    new_gqa_v = jnp.stack(l_gv, axis=1)
    new_diff_k = jnp.stack(l_dk, axis=1)
    new_diff_v = jnp.stack(l_dv, axis=1)
    return (y_prompt, y_sample, new_mla_ckv, new_mla_kpe, new_gqa_k, new_gqa_v, new_diff_k, new_diff_v)
```

```python
import functools
import math

import jax
import jax.numpy as jnp
from jax import lax
from jax.experimental import pallas as pl
from jax.experimental.pallas import tpu as pltpu

F32 = jnp.float32
BF16 = jnp.bfloat16
I32 = jnp.int32

D_MODEL = 1024
GRID_W = 64
ROPE_BASE = 10000.0
EPS = 1e-6
MLA_HEADS, MLA_NOPE, MLA_ROPE, MLA_V = 8, 64, 32, 64
MLA_Q_RANK, MLA_KV_RANK = 384, 256
GQA_HEADS, GQA_KV_HEADS, GQA_DIM = 8, 2, 64
DIFF_HEADS, DIFF_DIM = 4, 64
DIFF_V = 2 * DIFF_DIM
N_EXPERTS, EXPERT_FF, CAPACITY_FACTOR = 16, 1024, 2

LANES = 128
HEAD_PAD = 128
ROW_TILE = 256
Q_TILE = 256
FFN_TILE = 512
VMEM_LIMIT = 56 * 1024 * 1024

_O_CQ, _O_CKV, _O_GQ, _O_GK, _O_GV, _O_DQ, _O_DK, _O_DV, _O_KPE, _W1_COLS = (
    0, 384, 640, 1152, 1280, 1408, 1920, 2432, 2944, 3072)
_GQA_PERM = (0, 4, 1, 5, 2, 6, 3, 7)


def _cparams(sem):
    return pltpu.CompilerParams(dimension_semantics=sem, vmem_limit_bytes=VMEM_LIMIT)


def _dot(a, b):
    return jnp.dot(a, b, preferred_element_type=F32)


def _dot_nt(a, b):
    return lax.dot_general(a, b, (((1,), (1,)), ((), ())), preferred_element_type=F32)


def _rms(x, g):
    return x * lax.rsqrt(jnp.mean(x * x, axis=-1, keepdims=True) + EPS) * g


def _lane_iota(shape):
    return lax.broadcasted_iota(I32, shape, len(shape) - 1)


def _rope(x, cos, sin, width):
    outs = []
    even = ((_lane_iota((1, LANES)) // width) % 2) == 0
    for j in range(x.shape[1] // LANES):
        xb = x[:, j * LANES:(j + 1) * LANES]
        partner = jnp.where(even, pltpu.roll(xb, LANES - width, 1), pltpu.roll(xb, width, 1))
        outs.append(xb * cos + partner * sin)
    return outs[0] if len(outs) == 1 else jnp.concatenate(outs, axis=1)


def _group_mean_sq(x, bd):
    sq = x * x
    hi = sq.astype(BF16)
    lo = (sq - hi.astype(F32)).astype(BF16)
    return (_dot(hi, bd) + _dot(lo, bd)) * (1.0 / GQA_DIM)


def _ada_body(c_ref, w_ref, b_ref, o_ref):
    c = c_ref[...]
    s = c / (1.0 + jnp.exp(-c))
    o_ref[0] = _dot(s.astype(BF16), w_ref[0].astype(BF16)) + b_ref[0]


def _ada(cond, w_ada, b_ada):
    L, D, N = w_ada.shape
    R = cond.shape[0]
    tn = 512
    return pl.pallas_call(
        _ada_body,
        out_shape=jax.ShapeDtypeStruct((L, R, N), F32),
        grid=(L, N // tn),
        in_specs=[pl.BlockSpec((R, D), lambda l, j: (0, 0)),
                  pl.BlockSpec((1, D, tn), lambda l, j: (l, 0, j)),
                  pl.BlockSpec((1, 1, tn), lambda l, j: (l, 0, j))],
        out_specs=pl.BlockSpec((1, R, tn), lambda l, j: (l, 0, j)),
        compiler_params=_cparams(("parallel", "parallel")),
        name="ada",
    )(cond, w_ada, b_ada.reshape(L, 1, N))


def _in_body(rope, own, *refs):
    (x_ref, mod_ref, n1_ref, w_ref, qn_ref, kvn_ref, gqn_ref, gkn_ref, wuq_ref, wuk_ref, wuv_ref,
     bd_ref) = refs[:12]
    k = 12
    if rope:
        c64_ref, s64_ref, cm_ref, sm_ref = refs[k:k + 4]
        k += 4
    (qm_ref, km_ref, vm_ref, gq_ref, gk_ref, gv_ref, dq_ref, dk_ref, dv_ref) = refs[k:k + 9]
    k += 9
    if own:
        ockv_ref, okpe_ref, ogk_ref, ogv_ref, odk_ref, odv_ref = refs[k:k + 6]

    x = x_ref[...]
    shift = mod_ref[0, 0:1, :]
    scale = mod_ref[0, 1:2, :]
    h = _rms(x, n1_ref[...]) * (1.0 + scale) + shift
    z = _dot(h.astype(BF16), w_ref[...])
    bd = bd_ref[...]

    def rot64(u):
        return _rope(u, c64_ref[...], s64_ref[...], 16) if rope else u

    def rotm(u):
        return _rope(u, cm_ref[...], sm_ref[...], 8) if rope else u

    cq = _rms(z[:, _O_CQ:_O_CKV], qn_ref[...])
    q = _dot(cq.astype(BF16), wuq_ref[...])
    qm_ref[...] = (rotm(q) * ((MLA_NOPE + MLA_ROPE) ** -0.5)).astype(BF16)
    ckv = _rms(z[:, _O_CKV:_O_GQ], kvn_ref[...])
    kpe = z[:, _O_KPE:_W1_COLS]
    ckv_b = ckv.astype(BF16)
    kpe_at = pltpu.roll(kpe, MLA_NOPE, 1)
    knope = _dot(ckv_b, wuk_ref[...])
    kfull = knope + jnp.concatenate([kpe_at] * MLA_HEADS, axis=1)
    km_ref[...] = rotm(kfull).astype(BF16)
    vm_ref[...] = _dot(ckv_b, wuv_ref[...]).astype(BF16)
    gq = z[:, _O_GQ:_O_GK]
    gq = gq * lax.rsqrt(_group_mean_sq(gq, bd) + EPS) * gqn_ref[...]
    gq_ref[...] = (rot64(gq) * (GQA_DIM ** -0.5)).astype(BF16)
    gk = z[:, _O_GK:_O_GV]
    gk = gk * lax.rsqrt(_group_mean_sq(gk, bd[:LANES, :LANES]) + EPS) * gkn_ref[...]
    gk_ref[...] = rot64(gk).astype(BF16)
    gv = z[:, _O_GV:_O_DQ]
    gv_ref[...] = gv.astype(BF16)
    dq_ref[...] = (rot64(z[:, _O_DQ:_O_DK]) * (DIFF_DIM ** -0.5)).astype(BF16)
    dk = z[:, _O_DK:_O_DV]
    dk_ref[...] = rot64(dk).astype(BF16)
    dv = z[:, _O_DV:_O_KPE]
    dv_ref[...] = dv.astype(BF16)
    if own:
        ockv_ref[...] = ckv
        okpe_ref[...] = kpe[:, :MLA_ROPE]
        ogk_ref[...] = gk
        ogv_ref[...] = gv
        odk_ref[...] = dk
        odv_ref[...] = dv


def _in_proj(x, mod, lw, tabs, T, own):
    n, D = x.shape
    tm = ROW_TILE
    rope = tabs is not None
    tpb = T // tm
    nb = mod.shape[0]
    row = lambda i: (i, 0)
    const2 = lambda i: (0, 0)
    mod_map = (lambda i: (i // tpb, 0, 0)) if nb > 1 else (lambda i: (0, 0, 0))
    full = lambda a: pl.BlockSpec(a.shape, const2)
    ins = [x, mod, lw["norm1"], lw["w1"], lw["mla_q_norm"], lw["mla_kv_norm"], lw["gqa_q_norm"],
           lw["gqa_k_norm"], lw["wuq"], lw["wuk"], lw["wuv"], lw["bd"]]
    in_specs = [pl.BlockSpec((tm, D), row), pl.BlockSpec((1, 6, D), mod_map)] + [full(a) for a in ins[2:]]
    if rope:
        ins += list(tabs)
        in_specs += [pl.BlockSpec((tm, LANES), lambda i: (i % tpb, 0))] * 4
    widths = [(1024, BF16), (1024, BF16), (512, BF16), (512, BF16), (128, BF16), (128, BF16),
              (512, BF16), (512, BF16), (512, BF16)]
    if own:
        widths += [(256, F32), (32, F32), (128, F32), (128, F32), (512, F32), (512, F32)]
    out_shape = [jax.ShapeDtypeStruct((n, w), dt) for w, dt in widths]
    out_specs = [pl.BlockSpec((tm, w), row) for w, _ in widths]
    return pl.pallas_call(
        functools.partial(_in_body, rope, own),
        out_shape=out_shape, grid=(n // tm,), in_specs=in_specs, out_specs=out_specs,
        compiler_params=_cparams(("parallel",)), name="in_proj",
    )(*ins)


def _cache_body(ckv_ref, kpe_ref, wuk_ref, wuv_ref, km_ref, vm_ref):
    ckv_b = ckv_ref[...].astype(BF16)
    kpe_at = pltpu.roll(kpe_ref[...], MLA_NOPE, 1)
    km_ref[...] = (_dot(ckv_b, wuk_ref[...]) + jnp.concatenate([kpe_at] * MLA_HEADS, axis=1)).astype(BF16)
    vm_ref[...] = _dot(ckv_b, wuv_ref[...]).astype(BF16)


def _cache_proj(ckv, kpe_pad, lw):
    n = ckv.shape[0]
    tm = ROW_TILE
    row = lambda i: (i, 0)
    const2 = lambda i: (0, 0)
    return pl.pallas_call(
        _cache_body,
        out_shape=[jax.ShapeDtypeStruct((n, 1024), BF16), jax.ShapeDtypeStruct((n, 512), BF16)],
        grid=(n // tm,),
        in_specs=[pl.BlockSpec((tm, MLA_KV_RANK), row), pl.BlockSpec((tm, LANES), row),
                  pl.BlockSpec(lw["wuk"].shape, const2), pl.BlockSpec(lw["wuv"].shape, const2)],
        out_specs=[pl.BlockSpec((tm, 1024), row), pl.BlockSpec((tm, 512), row)],
        compiler_params=_cparams(("parallel",)), name="cache_proj",
    )(ckv, kpe_pad, lw["wuk"], lw["wuv"])


def _attn_body(mode, nparts, lam_init, *refs):
    q_ref = refs[0]
    kv = [(refs[1 + 2 * i], refs[2 + 2 * i]) for i in range(nparts)]
    k = 1 + 2 * nparts
    if mode == "diff":
        lq1, lk1, lq2, lk2, sub_ref = refs[k:k + 5]
        k += 5
    o_ref = refs[k]

    q = q_ref[0]
    lane = _lane_iota((1, LANES))
    lo = lane < (LANES // 2)
    if mode == "mla":
        qs = [q[:, :LANES], q[:, LANES:]]
    else:
        zero = jnp.zeros_like(q)
        qs = [jnp.where(lo, q, zero), jnp.where(lo, zero, q)]
    ks = [kr[0].astype(BF16) for kr, _ in kv]
    vs = [vr[0].astype(BF16) for _, vr in kv]
    outs = []
    for m in range(2):
        ss = []
        for kk in ks:
            kk = kk[:, m * LANES:(m + 1) * LANES] if mode == "mla" else kk
            ss.append(_dot_nt(qs[m], kk))
        mx = ss[0].max(axis=-1, keepdims=True)
        for s in ss[1:]:
            mx = jnp.maximum(mx, s.max(axis=-1, keepdims=True))
        den = None
        acc = None
        for s, v in zip(ss, vs):
            p = jnp.exp(s - mx)
            d = p.sum(axis=-1, keepdims=True)
            a = _dot(p.astype(BF16), v)
            den = d if den is None else den + d
            acc = a if acc is None else acc + a
        outs.append(acc / den)
    if mode == "diff":
        lam = (jnp.exp(jnp.sum(lq1[...] * lk1[...], axis=-1, keepdims=True))
               - jnp.exp(jnp.sum(lq2[...] * lk2[...], axis=-1, keepdims=True)) + lam_init)
        o = _rms(outs[0] - lam * outs[1], sub_ref[...]) * (1.0 - lam_init)
    else:
        o = jnp.where(lo, outs[0], outs[1])
    o_ref[0] = o.astype(o_ref.dtype)


def _attention(mode, q, parts, extra, lam_init=0.0):
    B, T, C = q.shape
    qw = 2 * LANES if mode == "mla" else LANES
    npair = C // qw
    tq = Q_TILE
    ins = [q]
    in_specs = [pl.BlockSpec((1, tq, qw), lambda b, g, i: (b, i, g))]
    for k, v in parts:
        Tk = k.shape[1]
        if mode == "gqa":
            kmap = vmap = lambda b, g, i: (b, 0, 0)
        else:
            kmap = vmap = lambda b, g, i: (b, 0, g)
        ins += [k, v]
        in_specs += [pl.BlockSpec((1, Tk, qw), kmap), pl.BlockSpec((1, Tk, LANES), vmap)]
    for e in extra:
        ins.append(e)
        in_specs.append(pl.BlockSpec(e.shape, lambda b, g, i: (0, 0)))
    return pl.pallas_call(
        functools.partial(_attn_body, mode, len(parts), lam_init),
        out_shape=jax.ShapeDtypeStruct((B, T, npair * LANES), BF16),
        grid=(B, npair, T // tq),
        in_specs=in_specs,
        out_specs=pl.BlockSpec((1, tq, LANES), lambda b, g, i: (b, i, g)),
        compiler_params=_cparams(("parallel", "parallel", "parallel")), name="attn_" + mode,
    )(*ins)


def _out_body(x_ref, mod_ref, n1_ref, n2_ref, om_ref, og_ref, od_ref, wg_ref, bm_ref, bg_ref, bdf_ref,
              wo_ref, rh_ref, rl_ref, x1_ref, h2_ref, aff_ref):
    x = x_ref[...]
    shift1, scale1, gate1 = mod_ref[0, 0:1, :], mod_ref[0, 1:2, :], mod_ref[0, 2:3, :]
    shift2, scale2 = mod_ref[0, 3:4, :], mod_ref[0, 4:5, :]
    h = (_rms(x, n1_ref[...]) * (1.0 + scale1) + shift1).astype(BF16)
    g = _dot(h, wg_ref[...])
    D = x.shape[1]
    sig = lambda u: 1.0 / (1.0 + jnp.exp(-u))
    merged = (sig(g[:, :D]) * _dot(om_ref[...], bm_ref[...])
              + sig(g[:, D:2 * D]) * _dot(og_ref[...], bg_ref[...])
              + sig(g[:, 2 * D:]) * _dot(od_ref[...], bdf_ref[...]))
    y = _dot(merged.astype(BF16), wo_ref[...])
    x1 = x + gate1 * y
    x1_ref[...] = x1
    h2 = _rms(x1, n2_ref[...]) * (1.0 + scale2) + shift2
    h2_ref[...] = h2
    hi = h2.astype(BF16)
    lo = (h2 - hi.astype(F32)).astype(BF16)
    logits = _dot(hi, rh_ref[...]) + _dot(lo, rh_ref[...]) + _dot(hi, rl_ref[...])
    valid = _lane_iota((1, LANES)) < N_EXPERTS
    logits = jnp.where(valid, logits, -1e30)
    e = jnp.exp(logits - logits.max(axis=-1, keepdims=True))
    aff_ref[...] = e / e.sum(axis=-1, keepdims=True)


def _out_proj(x, mod, om, og, od, lw, T):
    n, D = x.shape
    tm = ROW_TILE
    tpb = T // tm
    nb = mod.shape[0]
    row = lambda i: (i, 0)
    const2 = lambda i: (0, 0)
    mod_map = (lambda i: (i // tpb, 0, 0)) if nb > 1 else (lambda i: (0, 0, 0))
    full = lambda a: pl.BlockSpec(a.shape, const2)
    ws = [lw["wgate"], lw["wbr_mla"], lw["wbr_gqa"], lw["wbr_diff"], lw["w_out"], lw["wr_hi"], lw["wr_lo"]]
    return pl.pallas_call(
        _out_body,
        out_shape=[jax.ShapeDtypeStruct((n, D), F32), jax.ShapeDtypeStruct((n, D), F32),
                   jax.ShapeDtypeStruct((n, LANES), F32)],
        grid=(n // tm,),
        in_specs=[pl.BlockSpec((tm, D), row), pl.BlockSpec((1, 6, D), mod_map), full(lw["norm1"]),
                  full(lw["norm2"]), pl.BlockSpec((tm, 512), row), pl.BlockSpec((tm, 512), row),
                  pl.BlockSpec((tm, 512), row)] + [full(w) for w in ws],
        out_specs=[pl.BlockSpec((tm, D), row), pl.BlockSpec((tm, D), row), pl.BlockSpec((tm, LANES), row)],
        compiler_params=_cparams(("parallel",)), name="out_proj",
    )(x, mod, lw["norm1"], lw["norm2"], om, og, od, *ws)


def _route_body(cap, aff_ref, idx_ref, gate_ref):
    v = aff_ref[...]
    E, n = v.shape
    nbits = max(1, (n - 1).bit_length())
    lane = _lane_iota((E, n))

    def count(mask):
        return jnp.sum(mask.astype(I32), axis=1, keepdims=True)

    def step(i, t):
        cand = t | lax.shift_left(jnp.int32(1), 30 - i)
        return jnp.where(count(v >= lax.bitcast_convert_type(cand, F32)) >= cap, cand, t)

    thr = lax.bitcast_convert_type(lax.fori_loop(0, 31, step, jnp.zeros((E, 1), I32)), F32)

    def prefix(u):
        for b in range(nbits):
            s = 1 << b
            u = u + jnp.where(lane >= s, pltpu.roll(u, s, 1), 0)
        return u

    gt = v > thr
    eq = v == thr
    need = cap - count(gt)
    eq_i = eq.astype(I32)
    eq_rank = prefix(eq_i) - eq_i
    sel = jnp.where(gt | (eq & (eq_rank < need)), 1, 0).astype(I32)
    dist = lane + 1 - prefix(sel)
    tok = lane
    for b in range(nbits):
        s = 1 << b
        mv = sel * ((dist >> b) & 1)
        stay = sel - mv
        take = pltpu.roll(mv, n - s, 1) == 1
        tok = jnp.where(take, pltpu.roll(tok, n - s, 1), tok)
        v = jnp.where(take, pltpu.roll(v, n - s, 1), v)
        dist = jnp.where(take, pltpu.roll(dist, n - s, 1), dist)
        sel = jnp.where(take, 1, stay)
    idx_ref[...] = tok[:, :cap]
    gate_ref[...] = v[:, :cap]


def _route(aff_t, cap):
    E, n = aff_t.shape
    return pl.pallas_call(
        functools.partial(_route_body, cap),
        out_shape=[jax.ShapeDtypeStruct((E, cap), I32), jax.ShapeDtypeStruct((E, cap), F32)],
        compiler_params=pltpu.CompilerParams(vmem_limit_bytes=VMEM_LIMIT), name="route",
    )(aff_t)


def _gather_body(tc, tiles_per_e, idx_ref, xs_ref, o_ref, buf_ref):
    i = pl.program_id(0)
    e = i // tiles_per_e
    base = (i % tiles_per_e) * tc

    def body(s, _):
        r = idx_ref[e, base + s]
        buf_ref[pl.ds(s, 1), :] = xs_ref[pl.ds(r, 1), :]
        return 0

    lax.fori_loop(0, tc, body, 0, unroll=8)
    o_ref[...] = buf_ref[...].astype(BF16)


def _gather(idx, xs, tc):
    E, cap = idx.shape
    n, D = xs.shape
    tiles = cap // tc
    return pl.pallas_call(
        functools.partial(_gather_body, tc, tiles),
        out_shape=jax.ShapeDtypeStruct((E * cap, D), BF16),
        grid_spec=pltpu.PrefetchScalarGridSpec(
            num_scalar_prefetch=1, grid=(E * tiles,),
            in_specs=[pl.BlockSpec((n, D), lambda i, idx: (0, 0), pipeline_mode=pl.Buffered(1))],
            out_specs=pl.BlockSpec((tc, D), lambda i, idx: (i, 0)),
            scratch_shapes=[pltpu.VMEM((tc, D), F32)]),
        compiler_params=_cparams(("arbitrary",)), name="moe_gather",
    )(idx, xs)


def _ffn_body(xe_ref, wg_ref, wu_ref, wd_ref, o_ref):
    xe = xe_ref[...]
    hg = _dot(xe, wg_ref[0])
    hu = _dot(xe, wu_ref[0])
    a = (hg / (1.0 + jnp.exp(-hg))) * hu
    o_ref[...] = _dot(a.astype(BF16), wd_ref[0])


def _ffn(xe, wg, wu, wd, tc):
    E, D, FF = wg.shape
    rows = xe.shape[0]
    tiles = rows // E // tc
    wmap = lambda i: (i // tiles, 0, 0)
    return pl.pallas_call(
        _ffn_body,
        out_shape=jax.ShapeDtypeStruct((rows, D), F32),
        grid=(E * tiles,),
        in_specs=[pl.BlockSpec((tc, D), lambda i: (i, 0)), pl.BlockSpec((1, D, FF), wmap),
                  pl.BlockSpec((1, D, FF), wmap), pl.BlockSpec((1, FF, D), wmap)],
        out_specs=pl.BlockSpec((tc, D), lambda i: (i, 0)),
        compiler_params=_cparams(("parallel",)), name="moe_ffn",
    )(xe, wg, wu, wd)


def _combine_body(tc, tiles_per_e, n_scatter, final, idx_ref, gate_ref, ye_ref, x1_ref, mod_ref, fn_ref,
                  o_ref, acc_ref):
    i = pl.program_id(0)

    @pl.when(i == 0)
    def _():
        acc_ref[...] = jnp.zeros_like(acc_ref)

    @pl.when(i < n_scatter)
    def _():
        e = i // tiles_per_e
        base = (i % tiles_per_e) * tc

        def body(s, _):
            r = idx_ref[e, base + s]
            g = gate_ref[e, base + s]
            acc_ref[pl.ds(r, 1), :] += ye_ref[pl.ds(s, 1), :] * g
            return 0

        lax.fori_loop(0, tc, body, 0, unroll=8)

    @pl.when(i >= n_scatter)
    def _():
        tm = x1_ref.shape[0]
        j = i - n_scatter
        rows = pl.ds(pl.multiple_of(j * tm, tm), tm)
        xn = x1_ref[...] + mod_ref[0, 5:6, :] * acc_ref[rows, :]
        o_ref[...] = _rms(xn, fn_ref[...]) if final else xn


def _combine(idx, gate, ye, x1, mod, final_norm, tc, T, final):
    E, cap = idx.shape
    n, D = x1.shape
    tm = ROW_TILE
    tiles = cap // tc
    n_sc = E * tiles
    tpb = T // tm
    nb = mod.shape[0]
    out_row = lambda i, a, b: (jnp.maximum(i - n_sc, 0), 0)
    mod_map = ((lambda i, a, b: (jnp.maximum(i - n_sc, 0) // tpb, 0, 0)) if nb > 1
               else (lambda i, a, b: (0, 0, 0)))
    return pl.pallas_call(
        functools.partial(_combine_body, tc, tiles, n_sc, final),
        out_shape=jax.ShapeDtypeStruct((n, D), F32),
        grid_spec=pltpu.PrefetchScalarGridSpec(
            num_scalar_prefetch=2, grid=(n_sc + n // tm,),
            in_specs=[pl.BlockSpec((tc, D), lambda i, a, b: (jnp.minimum(i, n_sc - 1), 0)),
                      pl.BlockSpec((tm, D), out_row), pl.BlockSpec((1, 6, D), mod_map),
                      pl.BlockSpec((1, D), lambda i, a, b: (0, 0))],
            out_specs=pl.BlockSpec((tm, D), out_row),
            scratch_shapes=[pltpu.VMEM((n, D), F32)]),
        compiler_params=_cparams(("arbitrary",)), name="moe_combine",
    )(idx, gate, ye, x1, mod, final_norm)


def _rope_tables(T):
    t = jnp.arange(T)
    row = (t // GRID_W).astype(F32)
    col = (t % GRID_W).astype(F32)

    def table(head_w, lane_off, rep):
        half = head_w // 4
        freqs = ROPE_BASE ** (-jnp.arange(half, dtype=F32) / half)
        blk = []
        for pos in (row, col):
            ang = pos[:, None] * freqs[None, :]
            blk.append((jnp.cos(ang), jnp.sin(ang)))
        cos = jnp.concatenate([blk[0][0], blk[0][0], blk[1][0], blk[1][0]], axis=1)
        sin = jnp.concatenate([-blk[0][1], blk[0][1], -blk[1][1], blk[1][1]], axis=1)
        cos = jnp.concatenate([cos] * rep, axis=1)
        sin = jnp.concatenate([sin] * rep, axis=1)
        pad_r = LANES - lane_off - cos.shape[1]
        cos = jnp.pad(cos, ((0, 0), (lane_off, pad_r)), constant_values=1.0)
        sin = jnp.pad(sin, ((0, 0), (lane_off, pad_r)))
        return cos, sin

    c64, s64 = table(GQA_DIM, 0, 2)
    cm, sm = table(MLA_ROPE, MLA_NOPE, 1)
    return c64, s64, cm, sm


def _layer_weights(l, w_in, norm1, norm2, mla_q_norm, mla_kv_norm, mla_w_uq, mla_w_uk, mla_w_uv,
                   gqa_q_norm, gqa_k_norm, w_br_mla, w_br_gqa, w_br_diff, w_out, w_router):
    D = D_MODEL
    wi = w_in[l]
    seg = lambda a, b: wi[:, a:b]
    gq = seg(672, 1184).reshape(D, GQA_HEADS, GQA_DIM)[:, jnp.array(_GQA_PERM), :].reshape(D, 512)
    w1 = jnp.concatenate([seg(0, 384), seg(384, 640), gq, seg(1184, 1312), seg(1312, 1440), seg(1440, 1952),
                          seg(1952, 2464), seg(2464, 2976), seg(640, 672),
                          jnp.zeros((D, _W1_COLS - _O_KPE - MLA_ROPE), F32)], axis=1).astype(BF16)
    uq = mla_w_uq[l].reshape(MLA_Q_RANK, MLA_HEADS, MLA_NOPE + MLA_ROPE)
    wuq = jnp.pad(uq, ((0, 0), (0, 0), (0, HEAD_PAD - MLA_NOPE - MLA_ROPE))).reshape(MLA_Q_RANK, -1)
    wuk = jnp.pad(mla_w_uk[l], ((0, 0), (0, 0), (0, HEAD_PAD - MLA_NOPE))).reshape(MLA_KV_RANK, -1)
    wuv = mla_w_uv[l].reshape(MLA_KV_RANK, MLA_HEADS * MLA_V)
    grp = jnp.arange(512) // GQA_DIM
    bd = (grp[:, None] == grp[None, :]).astype(BF16)
    wbr_gqa = w_br_gqa[l].reshape(GQA_HEADS, GQA_DIM, D)[jnp.array(_GQA_PERM)].reshape(512, D)
    wr = jnp.pad(w_router[l], ((0, 0), (0, LANES - N_EXPERTS)))
    wr_hi = wr.astype(BF16)
    wr_lo = (wr - wr_hi.astype(F32)).astype(BF16)
    return {
        "norm1": norm1[l][None], "norm2": norm2[l][None], "w1": w1,
        "mla_q_norm": mla_q_norm[l][None], "mla_kv_norm": mla_kv_norm[l][None],
        "gqa_q_norm": jnp.tile(gqa_q_norm[l], GQA_HEADS)[None],
        "gqa_k_norm": jnp.tile(gqa_k_norm[l], GQA_KV_HEADS)[None],
        "wuq": wuq.astype(BF16), "wuk": wuk.astype(BF16), "wuv": wuv.astype(BF16), "bd": bd,
        "wgate": wi[:, 2976:].astype(BF16), "wbr_mla": w_br_mla[l].astype(BF16),
        "wbr_gqa": wbr_gqa.astype(BF16), "wbr_diff": w_br_diff[l].astype(BF16),
        "w_out": w_out[l].astype(BF16), "wr_hi": wr_hi, "wr_lo": wr_lo,
    }


def _trunk(x, mod, lw, ew, lam, lam_init, tabs, ctx, B, T, final_norm, final, own):
    n = B * T
    outs = _in_proj(x, mod, lw, tabs, T, own)
    qm, km, vm, gq, gk, gv, dq, dk, dv = [a.reshape(B, T, -1) for a in outs[:9]]
    if ctx is None:
        pm, pg, pd = [(km, vm)], [(gk, gv)], [(dk, dv)]
    else:
        ckm, cvm, cgk, cgv, cdk, cdv = ctx
        pm, pg, pd = [(ckm, cvm), (km, vm)], [(cgk, cgv), (gk, gv)], [(cdk, cdv), (dk, dv)]
    om = _attention("mla", qm, pm, []).reshape(n, -1)
    og = _attention("gqa", gq, pg, []).reshape(n, -1)
    od = _attention("diff", dq, pd, list(lam), lam_init).reshape(n, -1)
    x1, h2, aff = _out_proj(x, mod, om, og, od, lw, T)
    cap = CAPACITY_FACTOR * n // N_EXPERTS
    idx, gate = _route(aff[:, :N_EXPERTS].T, cap)
    tc = min(FFN_TILE, cap)
    xe = _gather(idx, h2, tc)
    ye = _ffn(xe, ew[0], ew[1], ew[2], tc)
    xn = _combine(idx, gate, ye, x1, mod, final_norm, tc, T, final)
    return xn, outs[9:]


def kernel(x_prompt, x_sample, cache_mla_ckv, cache_mla_kpe, cache_gqa_k, cache_gqa_v, cache_diff_k, cache_diff_v, c, c_ctx, w_ada, b_ada, norm1, norm2, w_in, mla_q_norm, mla_kv_norm, mla_w_uq, mla_w_uk, mla_w_uv, gqa_q_norm, gqa_k_norm, diff_lam_q1, diff_lam_k1, diff_lam_q2, diff_lam_k2, diff_subln, w_br_mla, w_br_gqa, w_br_diff, w_out, w_router, w_gate, w_up, w_down, final_norm):
    Bp, Tp, D = x_prompt.shape
    Bs, Ts, _ = x_sample.shape
    L = w_in.shape[0]
    P = cache_mla_ckv.shape[2]
    assert D == D_MODEL and Tp % ROW_TILE == 0 and Ts % ROW_TILE == 0 and Ts % GRID_W == 0

    nrow = -(-(1 + Bs) // 8) * 8
    cond = jnp.zeros((nrow, D), F32).at[0].set(c_ctx).at[1:1 + Bs].set(c)
    mod_all = _ada(cond, w_ada, b_ada).reshape(L, nrow, 6, D)

    tabs = _rope_tables(Ts)
    fn = final_norm[None]
    xp = x_prompt.reshape(Bp * Tp, D)
    xs = x_sample.reshape(Bs * Ts, D)
    owns = []
    for l in range(L):
        lw = _layer_weights(l, w_in, norm1, norm2, mla_q_norm, mla_kv_norm, mla_w_uq, mla_w_uk, mla_w_uv,
                            gqa_q_norm, gqa_k_norm, w_br_mla, w_br_gqa, w_br_diff, w_out, w_router)
        ew = (w_gate[l].astype(BF16), w_up[l].astype(BF16), w_down[l].astype(BF16))
        lam = (diff_lam_q1[l][None], diff_lam_k1[l][None], diff_lam_q2[l][None], diff_lam_k2[l][None],
               diff_subln[l][None])
        lam_init = 0.8 - 0.6 * math.exp(-0.3 * l)
        final = l == L - 1
        xp, own = _trunk(xp, mod_all[l, 0:1], lw, ew, lam, lam_init, None, None, Bp, Tp, fn, final, True)
        owns.append(own)
        kpe_pad = jnp.pad(cache_mla_kpe[:, l].reshape(Bs * P, MLA_ROPE), ((0, 0), (0, LANES - MLA_ROPE)))
        ckm, cvm = _cache_proj(cache_mla_ckv[:, l].reshape(Bs * P, MLA_KV_RANK), kpe_pad, lw)
        ctx = (ckm.reshape(Bs, P, -1), cvm.reshape(Bs, P, -1),
               cache_gqa_k[:, l].reshape(Bs, P, -1), cache_gqa_v[:, l].reshape(Bs, P, -1),
               cache_diff_k[:, l].reshape(Bs, P, -1), cache_diff_v[:, l].reshape(Bs, P, -1))
        xs, _ = _trunk(xs, mod_all[l, 1:1 + Bs], lw, ew, lam, lam_init, tabs, ctx, Bs, Ts, fn, final, False)

    def stack(k, shape):
        return jnp.stack([o[k].reshape((Bp, Tp) + shape) for o in owns], axis=1)

    return (xp.reshape(Bp, Tp, D), xs.reshape(Bs, Ts, D),
            stack(0, (MLA_KV_RANK,)), stack(1, (MLA_ROPE,)),
            stack(2, (GQA_KV_HEADS, GQA_DIM)), stack(3, (GQA_KV_HEADS, GQA_DIM)),
            stack(4, (DIFF_HEADS, 2 * DIFF_DIM)), stack(5, (DIFF_HEADS, DIFF_V)))
```

```python
import functools
import math

import jax
import jax.numpy as jnp
from jax import lax
from jax.experimental import pallas as pl
from jax.experimental.pallas import tpu as pltpu

F32 = jnp.float32
BF16 = jnp.bfloat16
I32 = jnp.int32

D_MODEL = 1024
GRID_W = 64
ROPE_BASE = 10000.0
EPS = 1e-6
MLA_HEADS, MLA_NOPE, MLA_ROPE, MLA_V = 8, 64, 32, 64
MLA_Q_RANK, MLA_KV_RANK = 384, 256
GQA_HEADS, GQA_KV_HEADS, GQA_DIM = 8, 2, 64
DIFF_HEADS, DIFF_DIM = 4, 64
DIFF_V = 2 * DIFF_DIM
N_EXPERTS, EXPERT_FF, CAPACITY_FACTOR = 16, 1024, 2

LANES = 128
SUBLANES = 8
HEAD_PAD = 128
ROW_TILE = 256
Q_TILE = 256
FFN_TILE = 512
VMEM_LIMIT = 56 * 1024 * 1024

_O_CQ, _O_CKV, _O_GQ, _O_GK, _O_GV, _O_DQ, _O_DK, _O_DV, _O_KPE, _W1_COLS = (
    0, 384, 640, 1152, 1280, 1408, 1920, 2432, 2944, 3072)
LOG2E = 1.4426950408889634


def _cparams(sem):
    return pltpu.CompilerParams(dimension_semantics=sem, vmem_limit_bytes=VMEM_LIMIT)


def _dot(a, b):
    return jnp.dot(a, b, preferred_element_type=F32)


def _dot_nt(a, b):
    return lax.dot_general(a, b, (((1,), (1,)), ((), ())), preferred_element_type=F32)


def _rms(x, g):
    return x * lax.rsqrt(jnp.mean(x * x, axis=-1, keepdims=True) + EPS) * g


def _lane_iota(shape):
    return lax.broadcasted_iota(I32, shape, len(shape) - 1)


def _rope(x, cos, sin, width):
    outs = []
    even = ((_lane_iota((1, LANES)) // width) % 2) == 0
    for j in range(x.shape[1] // LANES):
        xb = x[:, j * LANES:(j + 1) * LANES]
        partner = jnp.where(even, pltpu.roll(xb, LANES - width, 1), pltpu.roll(xb, width, 1))
        outs.append(xb * cos + partner * sin)
    return outs[0] if len(outs) == 1 else jnp.concatenate(outs, axis=1)


def _group_mean_sq(x, bd):
    sq = x * x
    hi = sq.astype(BF16)
    lo = (sq - hi.astype(F32)).astype(BF16)
    return (_dot(hi, bd) + _dot(lo, bd)) * (1.0 / GQA_DIM)


def _ada_body(c_ref, w_ref, b_ref, o_ref):
    c = c_ref[...]
    s = c / (1.0 + jnp.exp(-c))
    o_ref[0] = _dot(s.astype(BF16), w_ref[0].astype(BF16)) + b_ref[0]


def _ada(cond, w_ada, b_ada):
    L, D, N = w_ada.shape
    R = cond.shape[0]
    tn = 512
    return pl.pallas_call(
        _ada_body,
        out_shape=jax.ShapeDtypeStruct((L, R, N), F32),
        grid=(L, N // tn),
        in_specs=[pl.BlockSpec((R, D), lambda l, j: (0, 0)),
                  pl.BlockSpec((1, D, tn), lambda l, j: (l, 0, j)),
                  pl.BlockSpec((1, 1, tn), lambda l, j: (l, 0, j))],
        out_specs=pl.BlockSpec((1, R, tn), lambda l, j: (l, 0, j)),
        compiler_params=_cparams(("parallel", "parallel")),
        name="ada",
    )(cond, w_ada, b_ada.reshape(L, 1, N))


def _in_body(rope, own, *refs):
    (x_ref, mod_ref, n1_ref, w_ref, qn_ref, kvn_ref, gqn_ref, gkn_ref, wuq_ref, wuk_ref, wuv_ref,
     bd_ref) = refs[:12]
    k = 12
    if rope:
        c64_ref, s64_ref, cm_ref, sm_ref = refs[k:k + 4]
        k += 4
    (qm_ref, km_ref, vm_ref, gq_ref, gk_ref, gv_ref, dq_ref, dk_ref, dv_ref) = refs[k:k + 9]
    k += 9
    if own:
        ockv_ref, okpe_ref, ogk_ref, ogv_ref, odk_ref, odv_ref = refs[k:k + 6]

    x = x_ref[...]
    shift = mod_ref[0, 0:1, :]
    scale = mod_ref[0, 1:2, :]
    h = _rms(x, n1_ref[...]) * (1.0 + scale) + shift
    z = _dot(h.astype(BF16), w_ref[...])
    bd = bd_ref[...]

    def rot64(u):
        return _rope(u, c64_ref[...], s64_ref[...], 16) if rope else u

    def rotm(u):
        return _rope(u, cm_ref[...], sm_ref[...], 8) if rope else u

    cq = _rms(z[:, _O_CQ:_O_CKV], qn_ref[...])
    q = _dot(cq.astype(BF16), wuq_ref[...])
    qm_ref[...] = (rotm(q) * (LOG2E * (MLA_NOPE + MLA_ROPE) ** -0.5)).astype(BF16)
    ckv = _rms(z[:, _O_CKV:_O_GQ], kvn_ref[...])
    kpe = z[:, _O_KPE:_W1_COLS]
    ckv_b = ckv.astype(BF16)
    kpe_at = pltpu.roll(kpe, MLA_NOPE, 1)
    knope = _dot(ckv_b, wuk_ref[...])
    kfull = knope + jnp.concatenate([kpe_at] * MLA_HEADS, axis=1)
    km_ref[...] = rotm(kfull).astype(BF16)
    vm_ref[...] = _dot(ckv_b, wuv_ref[...]).astype(BF16)
    gq = z[:, _O_GQ:_O_GK]
    gq = gq * lax.rsqrt(_group_mean_sq(gq, bd) + EPS) * gqn_ref[...]
    gq_ref[...] = (rot64(gq) * (LOG2E * GQA_DIM ** -0.5)).astype(BF16)
    gk = z[:, _O_GK:_O_GV]
    gk = gk * lax.rsqrt(_group_mean_sq(gk, bd[:LANES, :LANES]) + EPS) * gkn_ref[...]
    gk_ref[...] = rot64(gk).astype(BF16)
    gv = z[:, _O_GV:_O_DQ]
    gv_ref[...] = gv.astype(BF16)
    dq_ref[...] = (rot64(z[:, _O_DQ:_O_DK]) * (LOG2E * DIFF_DIM ** -0.5)).astype(BF16)
    dk = z[:, _O_DK:_O_DV]
    dk_ref[...] = rot64(dk).astype(BF16)
    dv = z[:, _O_DV:_O_KPE]
    dv_ref[...] = dv.astype(BF16)
    if own:
        ockv_ref[...] = ckv
        okpe_ref[...] = kpe[:, :MLA_ROPE]
        ogk_ref[...] = gk
        ogv_ref[...] = gv
        odk_ref[...] = dk
        odv_ref[...] = dv


def _in_proj(x, mod, lw, tabs, T, own):
    n, D = x.shape
    tm = ROW_TILE
    rope = tabs is not None
    tpb = T // tm
    nb = mod.shape[0]
    row = lambda i: (i, 0)
    const2 = lambda i: (0, 0)
    mod_map = (lambda i: (i // tpb, 0, 0)) if nb > 1 else (lambda i: (0, 0, 0))
    full = lambda a: pl.BlockSpec(a.shape, const2)
    ins = [x, mod, lw["norm1"], lw["w1"], lw["mla_q_norm"], lw["mla_kv_norm"], lw["gqa_q_norm"],
           lw["gqa_k_norm"], lw["wuq"], lw["wuk"], lw["wuv"], lw["bd"]]
    in_specs = [pl.BlockSpec((tm, D), row), pl.BlockSpec((1, 6, D), mod_map)] + [full(a) for a in ins[2:]]
    if rope:
        ins += list(tabs)
        in_specs += [pl.BlockSpec((tm, LANES), lambda i: (i % tpb, 0))] * 4
    widths = [(1024, BF16), (1024, BF16), (512, BF16), (512, BF16), (128, BF16), (128, BF16),
              (512, BF16), (512, BF16), (512, BF16)]
    if own:
        widths += [(256, F32), (32, F32), (128, F32), (128, F32), (512, F32), (512, F32)]
    out_shape = [jax.ShapeDtypeStruct((n, w), dt) for w, dt in widths]
    out_specs = [pl.BlockSpec((tm, w), row) for w, _ in widths]
    return pl.pallas_call(
        functools.partial(_in_body, rope, own),
        out_shape=out_shape, grid=(n // tm,), in_specs=in_specs, out_specs=out_specs,
        compiler_params=_cparams(("parallel",)), name="in_proj",
    )(*ins)


def _cache_body(ckv_ref, kpe_ref, wuk_ref, wuv_ref, km_ref, vm_ref):
    ckv_b = ckv_ref[...].astype(BF16)
    kpe_at = pltpu.roll(kpe_ref[...], MLA_NOPE, 1)
    km_ref[...] = (_dot(ckv_b, wuk_ref[...]) + jnp.concatenate([kpe_at] * MLA_HEADS, axis=1)).astype(BF16)
    vm_ref[...] = _dot(ckv_b, wuv_ref[...]).astype(BF16)


def _cache_proj(ckv, kpe_pad, lw):
    n = ckv.shape[0]
    tm = ROW_TILE
    row = lambda i: (i, 0)
    const2 = lambda i: (0, 0)
    return pl.pallas_call(
        _cache_body,
        out_shape=[jax.ShapeDtypeStruct((n, 1024), BF16), jax.ShapeDtypeStruct((n, 512), BF16)],
        grid=(n // tm,),
        in_specs=[pl.BlockSpec((tm, MLA_KV_RANK), row), pl.BlockSpec((tm, LANES), row),
                  pl.BlockSpec(lw["wuk"].shape, const2), pl.BlockSpec(lw["wuv"].shape, const2)],
        out_specs=[pl.BlockSpec((tm, 1024), row), pl.BlockSpec((tm, 512), row)],
        compiler_params=_cparams(("parallel",)), name="cache_proj",
    )(ckv, kpe_pad, lw["wuk"], lw["wuv"])


def _swap_halves(a):
    return pltpu.roll(a.astype(F32), LANES // 2, 1).astype(a.dtype)


def _softmax_av(q, ks, vs, v_has_ones):
    ss = [_dot_nt(q, k) for k in ks]
    mx = ss[0].max(axis=-1, keepdims=True)
    for s in ss[1:]:
        mx = jnp.maximum(mx, s.max(axis=-1, keepdims=True))
    acc, den = None, None
    for s, v in zip(ss, vs):
        p = jnp.exp2(s - mx)
        if not v_has_ones:
            d = p.sum(axis=-1, keepdims=True)
            den = d if den is None else den + d
        a = _dot(p.astype(BF16), v)
        acc = a if acc is None else acc + a
    return acc, den


def _attn_body(mode, nparts, lam_init, *refs):
    q_ref = refs[0]
    kv = [(refs[1 + 2 * i], refs[2 + 2 * i]) for i in range(nparts)]
    k = 1 + 2 * nparts
    if mode == "diff":
        lq1, lk1, lq2, lk2, sub_ref = refs[k:k + 5]
        k += 5
    o_ref = refs[k]

    lo = _lane_iota((1, LANES)) < (LANES // 2)
    hi = jnp.logical_not(lo)
    ones = jnp.ones((1, LANES), BF16)
    cols = lambda j: slice(j * LANES, (j + 1) * LANES)

    def normalise(full):
        return full / pltpu.roll(full, LANES // 2, 1)

    if mode == "gqa":
        ks = [kr[0].astype(BF16) for kr, _ in kv]
        vs = [vr[0].astype(BF16) for _, vr in kv]
        ks_sw = [_swap_halves(a) for a in ks]
        vs_sw = [_swap_halves(a) for a in vs]
        for pair in range(GQA_HEADS // 2):
            qp = q_ref[0, :, cols(pair)]
            res = []
            for L in range(2):
                g = (2 * pair + L) // (GQA_HEADS // GQA_KV_HEADS)
                mine = lo if L == 0 else hi
                kk = ks if g == L else ks_sw
                vv = [jnp.where(mine, a, ones) for a in (vs if g == L else vs_sw)]
                full, _ = _softmax_av(jnp.where(mine, qp, jnp.zeros_like(qp)), kk, vv, True)
                res.append(normalise(full))
            o_ref[0, :, cols(pair)] = jnp.where(lo, res[0], res[1]).astype(o_ref.dtype)
    elif mode == "mla":
        for pair in range(MLA_HEADS // 2):
            vs = [vr[0, :, cols(pair)] for _, vr in kv]
            res = []
            for L in range(2):
                mine = lo if L == 0 else hi
                kk = [kr[0, :, cols(2 * pair + L)] for kr, _ in kv]
                vv = [jnp.where(mine, a, ones) for a in vs]
                full, _ = _softmax_av(q_ref[0, :, cols(2 * pair + L)], kk, vv, True)
                res.append(normalise(full))
            o_ref[0, :, cols(pair)] = jnp.where(lo, res[0], res[1]).astype(o_ref.dtype)
    else:
        lam = (jnp.exp(jnp.sum(lq1[...] * lk1[...], axis=-1, keepdims=True))
               - jnp.exp(jnp.sum(lq2[...] * lk2[...], axis=-1, keepdims=True)) + lam_init)
        for h in range(DIFF_HEADS):
            qh = q_ref[0, :, cols(h)]
            kk = [kr[0, :, cols(h)].astype(BF16) for kr, _ in kv]
            vv = [vr[0, :, cols(h)].astype(BF16) for _, vr in kv]
            zero = jnp.zeros_like(qh)
            a1, d1 = _softmax_av(jnp.where(lo, qh, zero), kk, vv, False)
            a2, d2 = _softmax_av(jnp.where(lo, zero, qh), kk, vv, False)
            o = _rms(a1 / d1 - lam * (a2 / d2), sub_ref[...]) * (1.0 - lam_init)
            o_ref[0, :, cols(h)] = o.astype(o_ref.dtype)


def _attention(mode, q, parts, extra, lam_init=0.0):
    B, T, C = q.shape
    tq = Q_TILE
    ins = [q]
    in_specs = [pl.BlockSpec((1, tq, C), lambda b, i: (b, i, 0))]
    for k, v in parts:
        ins += [k, v]
        in_specs += [pl.BlockSpec((1,) + k.shape[1:], lambda b, i: (b, 0, 0)),
                     pl.BlockSpec((1,) + v.shape[1:], lambda b, i: (b, 0, 0))]
    for e in extra:
        ins.append(e)
        in_specs.append(pl.BlockSpec(e.shape, lambda b, i: (0, 0)))
    return pl.pallas_call(
        functools.partial(_attn_body, mode, len(parts), lam_init),
        out_shape=jax.ShapeDtypeStruct((B, T, 512), BF16),
        grid=(B, T // tq),
        in_specs=in_specs,
        out_specs=pl.BlockSpec((1, tq, 512), lambda b, i: (b, i, 0)),
        compiler_params=_cparams(("parallel", "parallel")), name="attn_" + mode,
    )(*ins)


def _out_body(x_ref, mod_ref, n1_ref, n2_ref, om_ref, og_ref, od_ref, wg_ref, bm_ref, bg_ref, bdf_ref,
              wo_ref, rh_ref, rl_ref, x1_ref, h2_ref, aff_ref):
    x = x_ref[...]
    shift1, scale1, gate1 = mod_ref[0, 0:1, :], mod_ref[0, 1:2, :], mod_ref[0, 2:3, :]
    shift2, scale2 = mod_ref[0, 3:4, :], mod_ref[0, 4:5, :]
    h = (_rms(x, n1_ref[...]) * (1.0 + scale1) + shift1).astype(BF16)
    g = _dot(h, wg_ref[...])
    D = x.shape[1]
    sig = lambda u: 1.0 / (1.0 + jnp.exp(-u))
    merged = (sig(g[:, :D]) * _dot(om_ref[...], bm_ref[...])
              + sig(g[:, D:2 * D]) * _dot(og_ref[...], bg_ref[...])
              + sig(g[:, 2 * D:]) * _dot(od_ref[...], bdf_ref[...]))
    y = _dot(merged.astype(BF16), wo_ref[...])
    x1 = x + gate1 * y
    x1_ref[...] = x1
    h2 = _rms(x1, n2_ref[...]) * (1.0 + scale2) + shift2
    for cb in range(h2_ref.shape[0]):
        h2_ref[cb] = h2[:, cb * LANES:(cb + 1) * LANES]
    hi = h2.astype(BF16)
    lo = (h2 - hi.astype(F32)).astype(BF16)
    logits = _dot(hi, rh_ref[...]) + _dot(lo, rh_ref[...]) + _dot(hi, rl_ref[...])
    valid = _lane_iota((1, LANES)) < N_EXPERTS
    logits = jnp.where(valid, logits, -1e30)
    e = jnp.exp(logits - logits.max(axis=-1, keepdims=True))
    aff_ref[...] = e / e.sum(axis=-1, keepdims=True)


def _out_proj(x, mod, om, og, od, lw, T):
    n, D = x.shape
    tm = ROW_TILE
    tpb = T // tm
    nb = mod.shape[0]
    row = lambda i: (i, 0)
    const2 = lambda i: (0, 0)
    mod_map = (lambda i: (i // tpb, 0, 0)) if nb > 1 else (lambda i: (0, 0, 0))
    full = lambda a: pl.BlockSpec(a.shape, const2)
    ws = [lw["wgate"], lw["wbr_mla"], lw["wbr_gqa"], lw["wbr_diff"], lw["w_out"], lw["wr_hi"], lw["wr_lo"]]
    return pl.pallas_call(
        _out_body,
        out_shape=[jax.ShapeDtypeStruct((n, D), F32), jax.ShapeDtypeStruct((D // LANES, n, LANES), F32),
                   jax.ShapeDtypeStruct((n, LANES), F32)],
        grid=(n // tm,),
        in_specs=[pl.BlockSpec((tm, D), row), pl.BlockSpec((1, 6, D), mod_map), full(lw["norm1"]),
                  full(lw["norm2"]), pl.BlockSpec((tm, 512), row), pl.BlockSpec((tm, 512), row),
                  pl.BlockSpec((tm, 512), row)] + [full(w) for w in ws],
        out_specs=[pl.BlockSpec((tm, D), row), pl.BlockSpec((D // LANES, tm, LANES), lambda i: (0, i, 0)),
                   pl.BlockSpec((tm, LANES), row)],
        compiler_params=_cparams(("parallel",)), name="out_proj",
    )(x, mod, lw["norm1"], lw["norm2"], om, og, od, *ws)


def _route_body(cap, aff_ref, idx_ref, gate_ref):
    v = aff_ref[...]
    E, n = v.shape
    nbits = max(1, (n - 1).bit_length())
    lane = _lane_iota((E, n))

    def count(mask):
        return jnp.sum(mask.astype(I32), axis=1, keepdims=True)

    def step(i, t):
        cand = t | lax.shift_left(jnp.int32(1), 30 - i)
        return jnp.where(count(v >= lax.bitcast_convert_type(cand, F32)) >= cap, cand, t)

    thr = lax.bitcast_convert_type(lax.fori_loop(0, 31, step, jnp.zeros((E, 1), I32)), F32)

    def prefix(u):
        for b in range(nbits):
            s = 1 << b
            u = u + jnp.where(lane >= s, pltpu.roll(u, s, 1), 0)
        return u

    gt = v > thr
    eq = v == thr
    need = cap - count(gt)
    eq_i = eq.astype(I32)
    eq_rank = prefix(eq_i) - eq_i
    sel = jnp.where(gt | (eq & (eq_rank < need)), 1, 0).astype(I32)
    dist = lane + 1 - prefix(sel)
    tok = lane
    for b in range(nbits):
        s = 1 << b
        mv = sel * ((dist >> b) & 1)
        stay = sel - mv
        take = pltpu.roll(mv, n - s, 1) == 1
        tok = jnp.where(take, pltpu.roll(tok, n - s, 1), tok)
        v = jnp.where(take, pltpu.roll(v, n - s, 1), v)
        dist = jnp.where(take, pltpu.roll(dist, n - s, 1), dist)
        sel = jnp.where(take, 1, stay)
    idx_ref[...] = tok[:, :cap]
    gate_ref[...] = v[:, :cap]


def _route(aff_t, cap):
    E, n = aff_t.shape
    return pl.pallas_call(
        functools.partial(_route_body, cap),
        out_shape=[jax.ShapeDtypeStruct((E, cap), I32), jax.ShapeDtypeStruct((E, cap), F32)],
        compiler_params=pltpu.CompilerParams(vmem_limit_bytes=VMEM_LIMIT), name="route",
    )(aff_t)


def _gather_body(tc, tiles_per_e, idx_ref, xs_ref, o_ref):
    i = pl.program_id(0)
    e = i // tiles_per_e
    base = (i % tiles_per_e) * tc

    group = 16
    nblk = xs_ref.shape[0]
    sub = lax.broadcasted_iota(I32, (SUBLANES, LANES), 0)

    def body(g, _):
        accs = [[jnp.zeros((SUBLANES, LANES), F32) for _ in range(nblk)] for _ in range(group // SUBLANES)]
        for j in range(group):
            r = idx_ref[e, base + g * group + j]
            r8 = pl.multiple_of((r >> 3) << 3, SUBLANES)
            shift = (j - r) & (SUBLANES - 1)
            for cb in range(nblk):
                blk8 = pltpu.roll(xs_ref[cb, pl.ds(r8, SUBLANES), :], shift, 0)
                accs[j // SUBLANES][cb] = jnp.where(sub == j % SUBLANES, blk8, accs[j // SUBLANES][cb])
        rows = pl.ds(pl.multiple_of(g * group, group), group)
        for cb in range(nblk):
            o_ref[rows, cb * LANES:(cb + 1) * LANES] = jnp.concatenate(
                [a[cb] for a in accs], axis=0).astype(BF16)
        return 0

    lax.fori_loop(0, tc // group, body, 0)


def _gather(idx, xs, tc):
    E, cap = idx.shape
    nblk, n, _ = xs.shape
    D = nblk * LANES
    tiles = cap // tc
    return pl.pallas_call(
        functools.partial(_gather_body, tc, tiles),
        out_shape=jax.ShapeDtypeStruct((E * cap, D), BF16),
        grid_spec=pltpu.PrefetchScalarGridSpec(
            num_scalar_prefetch=1, grid=(E * tiles,),
            in_specs=[pl.BlockSpec((nblk, n, LANES), lambda i, idx: (0, 0, 0), pipeline_mode=pl.Buffered(1))],
            out_specs=pl.BlockSpec((tc, D), lambda i, idx: (i, 0))),
        compiler_params=_cparams(("arbitrary",)), name="moe_gather",
    )(idx, xs)


def _ffn_body(xe_ref, wg_ref, wu_ref, wd_ref, o_ref, wgb, wub, wdb):
    @pl.when(pl.program_id(1) == 0)
    def _():
        wgb[...] = wg_ref[0, 0].astype(BF16)
        wub[...] = wu_ref[0, 0].astype(BF16)
        wdb[...] = wd_ref[0, 0].astype(BF16)

    xe = xe_ref[...]
    hg = _dot(xe, wgb[...])
    hu = _dot(xe, wub[...])
    a = (hg / (1.0 + jnp.exp(-hg))) * hu
    o_ref[...] = _dot(a.astype(BF16), wdb[...])


def _ffn(xe, wg, wu, wd, l, tc):
    _, E, D, FF = wg.shape
    rows = xe.shape[0]
    tiles = rows // E // tc
    wmap = lambda e, t: (l, e, 0, 0)
    row = lambda e, t: (e * tiles + t, 0)
    return pl.pallas_call(
        _ffn_body,
        out_shape=jax.ShapeDtypeStruct((rows, D), F32),
        grid=(E, tiles),
        in_specs=[pl.BlockSpec((tc, D), row), pl.BlockSpec((1, 1, D, FF), wmap),
                  pl.BlockSpec((1, 1, D, FF), wmap), pl.BlockSpec((1, 1, FF, D), wmap)],
        out_specs=pl.BlockSpec((tc, D), row),
        scratch_shapes=[pltpu.VMEM((D, FF), BF16), pltpu.VMEM((D, FF), BF16), pltpu.VMEM((FF, D), BF16)],
        compiler_params=_cparams(("arbitrary", "arbitrary")), name="moe_ffn",
    )(xe, wg, wu, wd)


def _combine_body(tc, tiles_per_e, n_scatter, final, idx_ref, gate_ref, ye_ref, x1_ref, mod_ref, fn_ref,
                  o_ref, acc_ref):
    i = pl.program_id(0)

    @pl.when(i == 0)
    def _():
        acc_ref[...] = jnp.zeros_like(acc_ref)

    @pl.when(i < n_scatter)
    def _():
        e = i // tiles_per_e
        base = (i % tiles_per_e) * tc

        def body(s, _):
            r = idx_ref[e, base + s]
            g = gate_ref[e, base + s]
            acc_ref[pl.ds(r, 1), :] += ye_ref[pl.ds(s, 1), :] * g
            return 0

        lax.fori_loop(0, tc, body, 0, unroll=8)

    @pl.when(i >= n_scatter)
    def _():
        tm = x1_ref.shape[0]
        j = i - n_scatter
        rows = pl.ds(pl.multiple_of(j * tm, tm), tm)
        xn = x1_ref[...] + mod_ref[0, 5:6, :] * acc_ref[rows, :]
        o_ref[...] = _rms(xn, fn_ref[...]) if final else xn


def _combine(idx, gate, ye, x1, mod, final_norm, tc, T, final):
    E, cap = idx.shape
    n, D = x1.shape
    tm = ROW_TILE
    tiles = cap // tc
    n_sc = E * tiles
    tpb = T // tm
    nb = mod.shape[0]
    out_row = lambda i, a, b: (jnp.maximum(i - n_sc, 0), 0)
    mod_map = ((lambda i, a, b: (jnp.maximum(i - n_sc, 0) // tpb, 0, 0)) if nb > 1
               else (lambda i, a, b: (0, 0, 0)))
    return pl.pallas_call(
        functools.partial(_combine_body, tc, tiles, n_sc, final),
        out_shape=jax.ShapeDtypeStruct((n, D), F32),
        grid_spec=pltpu.PrefetchScalarGridSpec(
            num_scalar_prefetch=2, grid=(n_sc + n // tm,),
            in_specs=[pl.BlockSpec((tc, D), lambda i, a, b: (jnp.minimum(i, n_sc - 1), 0)),
                      pl.BlockSpec((tm, D), out_row), pl.BlockSpec((1, 6, D), mod_map),
                      pl.BlockSpec((1, D), lambda i, a, b: (0, 0))],
            out_specs=pl.BlockSpec((tm, D), out_row),
            scratch_shapes=[pltpu.VMEM((n, D), F32)]),
        compiler_params=_cparams(("arbitrary",)), name="moe_combine",
    )(idx, gate, ye, x1, mod, final_norm)


def _rope_tables(T):
    t = jnp.arange(T)
    row = (t // GRID_W).astype(F32)
    col = (t % GRID_W).astype(F32)

    def table(head_w, lane_off, rep):
        half = head_w // 4
        freqs = ROPE_BASE ** (-jnp.arange(half, dtype=F32) / half)
        blk = []
        for pos in (row, col):
            ang = pos[:, None] * freqs[None, :]
            blk.append((jnp.cos(ang), jnp.sin(ang)))
        cos = jnp.concatenate([blk[0][0], blk[0][0], blk[1][0], blk[1][0]], axis=1)
        sin = jnp.concatenate([-blk[0][1], blk[0][1], -blk[1][1], blk[1][1]], axis=1)
        cos = jnp.concatenate([cos] * rep, axis=1)
        sin = jnp.concatenate([sin] * rep, axis=1)
        pad_r = LANES - lane_off - cos.shape[1]
        cos = jnp.pad(cos, ((0, 0), (lane_off, pad_r)), constant_values=1.0)
        sin = jnp.pad(sin, ((0, 0), (lane_off, pad_r)))
        return cos, sin

    c64, s64 = table(GQA_DIM, 0, 2)
    cm, sm = table(MLA_ROPE, MLA_NOPE, 1)
    return c64, s64, cm, sm


def _layer_weights(l, w_in, norm1, norm2, mla_q_norm, mla_kv_norm, mla_w_uq, mla_w_uk, mla_w_uv,
                   gqa_q_norm, gqa_k_norm, w_br_mla, w_br_gqa, w_br_diff, w_out, w_router):
    D = D_MODEL
    wi = w_in[l]
    seg = lambda a, b: wi[:, a:b]
    w1 = jnp.concatenate([seg(0, 384), seg(384, 640), seg(672, 1184), seg(1184, 1312), seg(1312, 1440), seg(1440, 1952),
                          seg(1952, 2464), seg(2464, 2976), seg(640, 672),
                          jnp.zeros((D, _W1_COLS - _O_KPE - MLA_ROPE), F32)], axis=1).astype(BF16)
    uq = mla_w_uq[l].reshape(MLA_Q_RANK, MLA_HEADS, MLA_NOPE + MLA_ROPE)
    wuq = jnp.pad(uq, ((0, 0), (0, 0), (0, HEAD_PAD - MLA_NOPE - MLA_ROPE))).reshape(MLA_Q_RANK, -1)
    wuk = jnp.pad(mla_w_uk[l], ((0, 0), (0, 0), (0, HEAD_PAD - MLA_NOPE))).reshape(MLA_KV_RANK, -1)
    wuv = mla_w_uv[l].reshape(MLA_KV_RANK, MLA_HEADS * MLA_V)
    grp = jnp.arange(512) // GQA_DIM
    bd = (grp[:, None] == grp[None, :]).astype(BF16)
    wr = jnp.pad(w_router[l], ((0, 0), (0, LANES - N_EXPERTS)))
    wr_hi = wr.astype(BF16)
    wr_lo = (wr - wr_hi.astype(F32)).astype(BF16)
    return {
        "norm1": norm1[l][None], "norm2": norm2[l][None], "w1": w1,
        "mla_q_norm": mla_q_norm[l][None], "mla_kv_norm": mla_kv_norm[l][None],
        "gqa_q_norm": jnp.tile(gqa_q_norm[l], GQA_HEADS)[None],
        "gqa_k_norm": jnp.tile(gqa_k_norm[l], GQA_KV_HEADS)[None],
        "wuq": wuq.astype(BF16), "wuk": wuk.astype(BF16), "wuv": wuv.astype(BF16), "bd": bd,
        "wgate": wi[:, 2976:].astype(BF16), "wbr_mla": w_br_mla[l].astype(BF16),
        "wbr_gqa": w_br_gqa[l].astype(BF16), "wbr_diff": w_br_diff[l].astype(BF16),
        "w_out": w_out[l].astype(BF16), "wr_hi": wr_hi, "wr_lo": wr_lo,
    }


def _trunk(x, mod, lw, ew, lam, lam_init, tabs, ctx, B, T, final_norm, final, own):
    n = B * T
    outs = _in_proj(x, mod, lw, tabs, T, own)
    qm, km, vm, gq, gk, gv, dq, dk, dv = [a.reshape(B, T, -1) for a in outs[:9]]
    if ctx is None:
        pm, pg, pd = [(km, vm)], [(gk, gv)], [(dk, dv)]
    else:
        ckm, cvm, cgk, cgv, cdk, cdv = ctx
        pm, pg, pd = [(ckm, cvm), (km, vm)], [(cgk, cgv), (gk, gv)], [(cdk, cdv), (dk, dv)]
    om = _attention("mla", qm, pm, []).reshape(n, -1)
    og = _attention("gqa", gq, pg, []).reshape(n, -1)
    od = _attention("diff", dq, pd, list(lam), lam_init).reshape(n, -1)
    x1, h2, aff = _out_proj(x, mod, om, og, od, lw, T)
    cap = CAPACITY_FACTOR * n // N_EXPERTS
    idx, gate = _route(aff[:, :N_EXPERTS].T, cap)
    tc = min(FFN_TILE, cap)
    xe = _gather(idx, h2, tc)
    ye = _ffn(xe, ew[0], ew[1], ew[2], ew[3], tc)
    xn = _combine(idx, gate, ye, x1, mod, final_norm, tc, T, final)
    return xn, outs[9:]


def kernel(x_prompt, x_sample, cache_mla_ckv, cache_mla_kpe, cache_gqa_k, cache_gqa_v, cache_diff_k, cache_diff_v, c, c_ctx, w_ada, b_ada, norm1, norm2, w_in, mla_q_norm, mla_kv_norm, mla_w_uq, mla_w_uk, mla_w_uv, gqa_q_norm, gqa_k_norm, diff_lam_q1, diff_lam_k1, diff_lam_q2, diff_lam_k2, diff_subln, w_br_mla, w_br_gqa, w_br_diff, w_out, w_router, w_gate, w_up, w_down, final_norm):
    Bp, Tp, D = x_prompt.shape
    Bs, Ts, _ = x_sample.shape
    L = w_in.shape[0]
    P = cache_mla_ckv.shape[2]
    assert D == D_MODEL and Tp % ROW_TILE == 0 and Ts % ROW_TILE == 0 and Ts % GRID_W == 0

    nrow = -(-(1 + Bs) // 8) * 8
    cond = jnp.zeros((nrow, D), F32).at[0].set(c_ctx).at[1:1 + Bs].set(c)
    mod_all = _ada(cond, w_ada, b_ada).reshape(L, nrow, 6, D)

    tabs = _rope_tables(Ts)
    fn = final_norm[None]
    xp = x_prompt.reshape(Bp * Tp, D)
    xs = x_sample.reshape(Bs * Ts, D)
    owns = []
    for l in range(L):
        lw = _layer_weights(l, w_in, norm1, norm2, mla_q_norm, mla_kv_norm, mla_w_uq, mla_w_uk, mla_w_uv,
                            gqa_q_norm, gqa_k_norm, w_br_mla, w_br_gqa, w_br_diff, w_out, w_router)
        ew = (w_gate, w_up, w_down, l)
        lam = (diff_lam_q1[l][None], diff_lam_k1[l][None], diff_lam_q2[l][None], diff_lam_k2[l][None],
               diff_subln[l][None])
        lam_init = 0.8 - 0.6 * math.exp(-0.3 * l)
        final = l == L - 1
        xp, own = _trunk(xp, mod_all[l, 0:1], lw, ew, lam, lam_init, None, None, Bp, Tp, fn, final, True)
        owns.append(own)
        kpe_pad = jnp.pad(cache_mla_kpe[:, l].reshape(Bs * P, MLA_ROPE), ((0, 0), (0, LANES - MLA_ROPE)))
        ckm, cvm = _cache_proj(cache_mla_ckv[:, l].reshape(Bs * P, MLA_KV_RANK), kpe_pad, lw)
        ctx = (ckm.reshape(Bs, P, -1), cvm.reshape(Bs, P, -1),
               cache_gqa_k[:, l].reshape(Bs, P, -1), cache_gqa_v[:, l].reshape(Bs, P, -1),
               cache_diff_k[:, l].reshape(Bs, P, -1), cache_diff_v[:, l].reshape(Bs, P, -1))
        xs, _ = _trunk(xs, mod_all[l, 1:1 + Bs], lw, ew, lam, lam_init, tabs, ctx, Bs, Ts, fn, final, False)

    def stack(k, shape):
        return jnp.stack([o[k].reshape((Bp, Tp) + shape) for o in owns], axis=1)

    return (xp.reshape(Bp, Tp, D), xs.reshape(Bs, Ts, D),
            stack(0, (MLA_KV_RANK,)), stack(1, (MLA_ROPE,)),
            stack(2, (GQA_KV_HEADS, GQA_DIM)), stack(3, (GQA_KV_HEADS, GQA_DIM)),
            stack(4, (DIFF_HEADS, 2 * DIFF_DIM)), stack(5, (DIFF_HEADS, DIFF_V)))
```

```python
import functools
import math

import jax
import jax.numpy as jnp
from jax import lax
from jax.experimental import pallas as pl
from jax.experimental.pallas import tpu as pltpu

F32 = jnp.float32
BF16 = jnp.bfloat16
I32 = jnp.int32

D_MODEL = 1024
GRID_W = 64
ROPE_BASE = 10000.0
EPS = 1e-6
MLA_HEADS, MLA_NOPE, MLA_ROPE, MLA_V = 8, 64, 32, 64
MLA_Q_RANK, MLA_KV_RANK = 384, 256
GQA_HEADS, GQA_KV_HEADS, GQA_DIM = 8, 2, 64
DIFF_HEADS, DIFF_DIM = 4, 64
DIFF_V = 2 * DIFF_DIM
N_EXPERTS, EXPERT_FF, CAPACITY_FACTOR = 16, 1024, 2

LANES = 128
SUBLANES = 8
HEAD_PAD = 128
ROW_TILE = 256
Q_TILE = 512
FFN_TILE = 1024
MOE_ROW_TILE = 512
VMEM_LIMIT = 56 * 1024 * 1024

_O_CQ, _O_CKV, _O_GQ, _O_GK, _O_GV, _O_DQ, _O_DK, _O_DV, _O_KPE, _W1_COLS = (
    0, 384, 640, 1152, 1280, 1408, 1920, 2432, 2944, 3072)
LOG2E = 1.4426950408889634


def _cparams(sem):
    return pltpu.CompilerParams(dimension_semantics=sem, vmem_limit_bytes=VMEM_LIMIT)


def _dot(a, b):
    return jnp.dot(a, b, preferred_element_type=F32)


def _dot_nt(a, b):
    return lax.dot_general(a, b, (((1,), (1,)), ((), ())), preferred_element_type=F32)


def _rms(x, g):
    return x * lax.rsqrt(jnp.mean(x * x, axis=-1, keepdims=True) + EPS) * g


def _lane_iota(shape):
    return lax.broadcasted_iota(I32, shape, len(shape) - 1)


def _rope(x, cos, sin, width):
    outs = []
    even = ((_lane_iota((1, LANES)) // width) % 2) == 0
    for j in range(x.shape[1] // LANES):
        xb = x[:, j * LANES:(j + 1) * LANES]
        partner = jnp.where(even, pltpu.roll(xb, LANES - width, 1), pltpu.roll(xb, width, 1))
        outs.append(xb * cos + partner * sin)
    return outs[0] if len(outs) == 1 else jnp.concatenate(outs, axis=1)


def _group_mean_sq(x, bd):
    sq = x * x
    hi = sq.astype(BF16)
    lo = (sq - hi.astype(F32)).astype(BF16)
    return (_dot(hi, bd) + _dot(lo, bd)) * (1.0 / GQA_DIM)


def _ada_body(c_ref, w_ref, b_ref, o_ref):
    c = c_ref[...]
    s = c / (1.0 + jnp.exp(-c))
    o_ref[0] = _dot(s.astype(BF16), w_ref[0].astype(BF16)) + b_ref[0]


def _ada(cond, w_ada, b_ada):
    L, D, N = w_ada.shape
    R = cond.shape[0]
    tn = 512
    return pl.pallas_call(
        _ada_body,
        out_shape=jax.ShapeDtypeStruct((L, R, N), F32),
        grid=(L, N // tn),
        in_specs=[pl.BlockSpec((R, D), lambda l, j: (0, 0)),
                  pl.BlockSpec((1, D, tn), lambda l, j: (l, 0, j)),
                  pl.BlockSpec((1, 1, tn), lambda l, j: (l, 0, j))],
        out_specs=pl.BlockSpec((1, R, tn), lambda l, j: (l, 0, j)),
        compiler_params=_cparams(("parallel", "parallel")),
        name="ada",
    )(cond, w_ada, b_ada.reshape(L, 1, N))


def _in_body(rope, own, *refs):
    (x_ref, mod_ref, n1_ref, w_ref, qn_ref, kvn_ref, gqn_ref, gkn_ref, wuq_ref, wuk_ref, wuv_ref,
     bd_ref) = refs[:12]
    k = 12
    if rope:
        c64_ref, s64_ref, cm_ref, sm_ref = refs[k:k + 4]
        k += 4
    (qm_ref, km_ref, vm_ref, gq_ref, gk_ref, gv_ref, dq_ref, dk_ref, dv_ref) = refs[k:k + 9]
    k += 9
    if own:
        ockv_ref, okpe_ref, ogk_ref, ogv_ref, odk_ref, odv_ref = refs[k:k + 6]

    x = x_ref[...]
    shift = mod_ref[0, 0:1, :]
    scale = mod_ref[0, 1:2, :]
    h = _rms(x, n1_ref[...]) * (1.0 + scale) + shift
    z = _dot(h.astype(BF16), w_ref[...])
    bd = bd_ref[...]

    def rot64(u):
        return _rope(u, c64_ref[...], s64_ref[...], 16) if rope else u

    def rotm(u):
        return _rope(u, cm_ref[...], sm_ref[...], 8) if rope else u

    cq = _rms(z[:, _O_CQ:_O_CKV], qn_ref[...])
    q = _dot(cq.astype(BF16), wuq_ref[...])
    qm_ref[...] = (rotm(q) * (LOG2E * (MLA_NOPE + MLA_ROPE) ** -0.5)).astype(BF16)
    ckv = _rms(z[:, _O_CKV:_O_GQ], kvn_ref[...])
    kpe = z[:, _O_KPE:_W1_COLS]
    ckv_b = ckv.astype(BF16)
    kpe_at = pltpu.roll(kpe, MLA_NOPE, 1)
    knope = _dot(ckv_b, wuk_ref[...])
    kfull = knope + jnp.concatenate([kpe_at] * MLA_HEADS, axis=1)
    km_ref[...] = rotm(kfull).astype(BF16)
    vm_ref[...] = _dot(ckv_b, wuv_ref[...]).astype(BF16)
    gq = z[:, _O_GQ:_O_GK]
    gq = gq * lax.rsqrt(_group_mean_sq(gq, bd) + EPS) * gqn_ref[...]
    gq_ref[...] = (rot64(gq) * (LOG2E * GQA_DIM ** -0.5)).astype(BF16)
    gk = z[:, _O_GK:_O_GV]
    gk = gk * lax.rsqrt(_group_mean_sq(gk, bd[:LANES, :LANES]) + EPS) * gkn_ref[...]
    gk_ref[...] = rot64(gk).astype(BF16)
    gv = z[:, _O_GV:_O_DQ]
    gv_ref[...] = gv.astype(BF16)
    dq_ref[...] = (rot64(z[:, _O_DQ:_O_DK]) * (LOG2E * DIFF_DIM ** -0.5)).astype(BF16)
    dk = z[:, _O_DK:_O_DV]
    dk_ref[...] = rot64(dk).astype(BF16)
    dv = z[:, _O_DV:_O_KPE]
    dv_ref[...] = dv.astype(BF16)
    if own:
        ockv_ref[...] = ckv
        okpe_ref[...] = kpe[:, :MLA_ROPE]
        ogk_ref[...] = gk
        ogv_ref[...] = gv
        odk_ref[...] = dk
        odv_ref[...] = dv


def _in_proj(x, mod, lw, tabs, T, own):
    n, D = x.shape
    tm = ROW_TILE
    rope = tabs is not None
    tpb = T // tm
    nb = mod.shape[0]
    row = lambda i: (i, 0)
    const2 = lambda i: (0, 0)
    mod_map = (lambda i: (i // tpb, 0, 0)) if nb > 1 else (lambda i: (0, 0, 0))
    full = lambda a: pl.BlockSpec(a.shape, const2)
    ins = [x, mod, lw["norm1"], lw["w1"], lw["mla_q_norm"], lw["mla_kv_norm"], lw["gqa_q_norm"],
           lw["gqa_k_norm"], lw["wuq"], lw["wuk"], lw["wuv"], lw["bd"]]
    in_specs = [pl.BlockSpec((tm, D), row), pl.BlockSpec((1, 6, D), mod_map)] + [full(a) for a in ins[2:]]
    if rope:
        ins += list(tabs)
        in_specs += [pl.BlockSpec((tm, LANES), lambda i: (i % tpb, 0))] * 4
    widths = [(1024, BF16), (1024, BF16), (512, BF16), (512, BF16), (128, BF16), (128, BF16),
              (512, BF16), (512, BF16), (512, BF16)]
    if own:
        widths += [(256, F32), (32, F32), (128, F32), (128, F32), (512, F32), (512, F32)]
    out_shape = [jax.ShapeDtypeStruct((n, w), dt) for w, dt in widths]
    out_specs = [pl.BlockSpec((tm, w), row) for w, _ in widths]
    return pl.pallas_call(
        functools.partial(_in_body, rope, own),
        out_shape=out_shape, grid=(n // tm,), in_specs=in_specs, out_specs=out_specs,
        compiler_params=_cparams(("parallel",)), name="in_proj",
    )(*ins)


def _cache_body(ckv_ref, kpe_ref, wuk_ref, wuv_ref, km_ref, vm_ref):
    ckv_b = ckv_ref[...].astype(BF16)
    kpe_at = pltpu.roll(kpe_ref[...], MLA_NOPE, 1)
    km_ref[...] = (_dot(ckv_b, wuk_ref[...]) + jnp.concatenate([kpe_at] * MLA_HEADS, axis=1)).astype(BF16)
    vm_ref[...] = _dot(ckv_b, wuv_ref[...]).astype(BF16)


def _cache_proj(ckv, kpe_pad, lw):
    n = ckv.shape[0]
    tm = ROW_TILE
    row = lambda i: (i, 0)
    const2 = lambda i: (0, 0)
    return pl.pallas_call(
        _cache_body,
        out_shape=[jax.ShapeDtypeStruct((n, 1024), BF16), jax.ShapeDtypeStruct((n, 512), BF16)],
        grid=(n // tm,),
        in_specs=[pl.BlockSpec((tm, MLA_KV_RANK), row), pl.BlockSpec((tm, LANES), row),
                  pl.BlockSpec(lw["wuk"].shape, const2), pl.BlockSpec(lw["wuv"].shape, const2)],
        out_specs=[pl.BlockSpec((tm, 1024), row), pl.BlockSpec((tm, 512), row)],
        compiler_params=_cparams(("parallel",)), name="cache_proj",
    )(ckv, kpe_pad, lw["wuk"], lw["wuv"])


def _swap_halves(a):
    return pltpu.roll(a.astype(F32), LANES // 2, 1).astype(a.dtype)


def _softmax_av(q, ks, vs, v_has_ones):
    ss = [_dot_nt(q, k) for k in ks]
    mx = ss[0].max(axis=-1, keepdims=True)
    for s in ss[1:]:
        mx = jnp.maximum(mx, s.max(axis=-1, keepdims=True))
    acc, den = None, None
    for s, v in zip(ss, vs):
        p = jnp.exp2(s - mx)
        if not v_has_ones:
            d = p.sum(axis=-1, keepdims=True)
            den = d if den is None else den + d
        a = _dot(p.astype(BF16), v)
        acc = a if acc is None else acc + a
    return acc, den


def _attn_body(mode, nparts, lam_init, *refs):
    q_ref = refs[0]
    kv = [(refs[1 + 2 * i], refs[2 + 2 * i]) for i in range(nparts)]
    k = 1 + 2 * nparts
    if mode == "diff":
        lq1, lk1, lq2, lk2, sub_ref = refs[k:k + 5]
        k += 5
    o_ref = refs[k]

    lo = _lane_iota((1, LANES)) < (LANES // 2)
    hi = jnp.logical_not(lo)
    ones = jnp.ones((1, LANES), BF16)
    cols = lambda j: slice(j * LANES, (j + 1) * LANES)

    def normalise(full):
        return full / pltpu.roll(full, LANES // 2, 1)

    if mode == "gqa":
        ks = [kr[0].astype(BF16) for kr, _ in kv]
        vs = [vr[0].astype(BF16) for _, vr in kv]
        ks_sw = [_swap_halves(a) for a in ks]
        vs_sw = [_swap_halves(a) for a in vs]
        for pair in range(GQA_HEADS // 2):
            qp = q_ref[0, :, cols(pair)]
            res = []
            for L in range(2):
                g = (2 * pair + L) // (GQA_HEADS // GQA_KV_HEADS)
                mine = lo if L == 0 else hi
                kk = ks if g == L else ks_sw
                vv = [jnp.where(mine, a, ones) for a in (vs if g == L else vs_sw)]
                full, _ = _softmax_av(jnp.where(mine, qp, jnp.zeros_like(qp)), kk, vv, True)
                res.append(normalise(full))
            o_ref[0, :, cols(pair)] = jnp.where(lo, res[0], res[1]).astype(o_ref.dtype)
    elif mode == "mla":
        for pair in range(MLA_HEADS // 2):
            vs = [vr[0, :, cols(pair)] for _, vr in kv]
            res = []
            for L in range(2):
                mine = lo if L == 0 else hi
                kk = [kr[0, :, cols(2 * pair + L)] for kr, _ in kv]
                vv = [jnp.where(mine, a, ones) for a in vs]
                full, _ = _softmax_av(q_ref[0, :, cols(2 * pair + L)], kk, vv, True)
                res.append(normalise(full))
            o_ref[0, :, cols(pair)] = jnp.where(lo, res[0], res[1]).astype(o_ref.dtype)
    else:
        lam = (jnp.exp(jnp.sum(lq1[...] * lk1[...], axis=-1, keepdims=True))
               - jnp.exp(jnp.sum(lq2[...] * lk2[...], axis=-1, keepdims=True)) + lam_init)
        for h in range(DIFF_HEADS):
            qh = q_ref[0, :, cols(h)]
            kk = [kr[0, :, cols(h)].astype(BF16) for kr, _ in kv]
            vv = [vr[0, :, cols(h)].astype(BF16) for _, vr in kv]
            zero = jnp.zeros_like(qh)
            a1, d1 = _softmax_av(jnp.where(lo, qh, zero), kk, vv, False)
            a2, d2 = _softmax_av(jnp.where(lo, zero, qh), kk, vv, False)
            o = _rms(a1 / d1 - lam * (a2 / d2), sub_ref[...]) * (1.0 - lam_init)
            o_ref[0, :, cols(h)] = o.astype(o_ref.dtype)


def _attention(mode, q, parts, extra, lam_init=0.0):
    B, T, C = q.shape
    tq = min(Q_TILE, T)
    ins = [q]
    in_specs = [pl.BlockSpec((1, tq, C), lambda b, i: (b, i, 0))]
    for k, v in parts:
        ins += [k, v]
        in_specs += [pl.BlockSpec((1,) + k.shape[1:], lambda b, i: (b, 0, 0)),
                     pl.BlockSpec((1,) + v.shape[1:], lambda b, i: (b, 0, 0))]
    for e in extra:
        ins.append(e)
        in_specs.append(pl.BlockSpec(e.shape, lambda b, i: (0, 0)))
    return pl.pallas_call(
        functools.partial(_attn_body, mode, len(parts), lam_init),
        out_shape=jax.ShapeDtypeStruct((B, T, 512), BF16),
        grid=(B, T // tq),
        in_specs=in_specs,
        out_specs=pl.BlockSpec((1, tq, 512), lambda b, i: (b, i, 0)),
        compiler_params=_cparams(("parallel", "parallel")), name="attn_" + mode,
    )(*ins)


def _out_body(x_ref, mod_ref, n1_ref, n2_ref, om_ref, og_ref, od_ref, wg_ref, bm_ref, bg_ref, bdf_ref,
              wo_ref, rh_ref, rl_ref, x1_ref, h2_ref, aff_ref):
    x = x_ref[...]
    shift1, scale1, gate1 = mod_ref[0, 0:1, :], mod_ref[0, 1:2, :], mod_ref[0, 2:3, :]
    shift2, scale2 = mod_ref[0, 3:4, :], mod_ref[0, 4:5, :]
    h = (_rms(x, n1_ref[...]) * (1.0 + scale1) + shift1).astype(BF16)
    g = _dot(h, wg_ref[...])
    D = x.shape[1]
    sig = lambda u: 1.0 / (1.0 + jnp.exp(-u))
    merged = (sig(g[:, :D]) * _dot(om_ref[...], bm_ref[...])
              + sig(g[:, D:2 * D]) * _dot(og_ref[...], bg_ref[...])
              + sig(g[:, 2 * D:]) * _dot(od_ref[...], bdf_ref[...]))
    y = _dot(merged.astype(BF16), wo_ref[...])
    x1 = x + gate1 * y
    x1_ref[...] = x1
    h2 = _rms(x1, n2_ref[...]) * (1.0 + scale2) + shift2
    h2_ref[...] = h2
    hi = h2.astype(BF16)
    lo = (h2 - hi.astype(F32)).astype(BF16)
    logits = _dot(hi, rh_ref[...]) + _dot(lo, rh_ref[...]) + _dot(hi, rl_ref[...])
    valid = _lane_iota((1, LANES)) < N_EXPERTS
    logits = jnp.where(valid, logits, -1e30)
    e = jnp.exp(logits - logits.max(axis=-1, keepdims=True))
    aff = e / e.sum(axis=-1, keepdims=True)
    aff_ref[...] = aff.T[:N_EXPERTS, :]


def _out_proj(x, mod, om, og, od, lw, T):
    n, D = x.shape
    tm = ROW_TILE
    tpb = T // tm
    nb = mod.shape[0]
    row = lambda i: (i, 0)
    const2 = lambda i: (0, 0)
    mod_map = (lambda i: (i // tpb, 0, 0)) if nb > 1 else (lambda i: (0, 0, 0))
    full = lambda a: pl.BlockSpec(a.shape, const2)
    ws = [lw["wgate"], lw["wbr_mla"], lw["wbr_gqa"], lw["wbr_diff"], lw["w_out"], lw["wr_hi"], lw["wr_lo"]]
    return pl.pallas_call(
        _out_body,
        out_shape=[jax.ShapeDtypeStruct((n, D), F32), jax.ShapeDtypeStruct((n, D), F32),
                   jax.ShapeDtypeStruct((N_EXPERTS, n), F32)],
        grid=(n // tm,),
        in_specs=[pl.BlockSpec((tm, D), row), pl.BlockSpec((1, 6, D), mod_map), full(lw["norm1"]),
                  full(lw["norm2"]), pl.BlockSpec((tm, 512), row), pl.BlockSpec((tm, 512), row),
                  pl.BlockSpec((tm, 512), row)] + [full(w) for w in ws],
        out_specs=[pl.BlockSpec((tm, D), row), pl.BlockSpec((tm, D), row),
                   pl.BlockSpec((N_EXPERTS, tm), lambda i: (0, i))],
        compiler_params=_cparams(("parallel",)), name="out_proj",
    )(x, mod, lw["norm1"], lw["norm2"], om, og, od, *ws)


def _route_body(cap, aff_ref, idx_ref, gate_ref):
    v = aff_ref[...]
    E, n = v.shape
    nbits = max(1, (n - 1).bit_length())
    lane = _lane_iota((E, n))

    def count(mask):
        return jnp.sum(mask.astype(I32), axis=1, keepdims=True)

    def step(i, t):
        cand = t | lax.shift_left(jnp.int32(1), 30 - i)
        return jnp.where(count(v >= lax.bitcast_convert_type(cand, F32)) >= cap, cand, t)

    thr = lax.bitcast_convert_type(lax.fori_loop(0, 31, step, jnp.zeros((E, 1), I32)), F32)

    def prefix(u):
        for b in range(nbits):
            s = 1 << b
            u = u + jnp.where(lane >= s, pltpu.roll(u, s, 1), 0)
        return u

    gt = v > thr
    eq = v == thr
    need = cap - count(gt)
    eq_i = eq.astype(I32)
    eq_rank = prefix(eq_i) - eq_i
    sel = jnp.where(gt | (eq & (eq_rank < need)), 1, 0).astype(I32)
    dist = lane + 1 - prefix(sel)
    tok = lane
    for b in range(nbits):
        s = 1 << b
        mv = sel * ((dist >> b) & 1)
        stay = sel - mv
        take = pltpu.roll(mv, n - s, 1) == 1
        tok = jnp.where(take, pltpu.roll(tok, n - s, 1), tok)
        v = jnp.where(take, pltpu.roll(v, n - s, 1), v)
        dist = jnp.where(take, pltpu.roll(dist, n - s, 1), dist)
        sel = jnp.where(take, 1, stay)
    idx_ref[...] = tok[:, :cap]
    gate_ref[...] = v[:, :cap]


def _route(aff_t, cap):
    E, n = aff_t.shape
    return pl.pallas_call(
        functools.partial(_route_body, cap),
        out_shape=[jax.ShapeDtypeStruct((E, cap), I32), jax.ShapeDtypeStruct((E, cap), F32)],
        compiler_params=pltpu.CompilerParams(vmem_limit_bytes=VMEM_LIMIT), name="route",
    )(aff_t)


def _gather_body(tc, tiles_per_e, idx_ref, xs_ref, o_ref):
    i = pl.program_id(0)
    base = i * tc

    group = 16
    D = xs_ref.shape[1]
    sub = lax.broadcasted_iota(I32, (SUBLANES, D), 0)

    def body(g, _):
        halves = []
        for hf in range(group // SUBLANES):
            acc = jnp.zeros((SUBLANES, D), F32)
            for j in range(SUBLANES):
                r = idx_ref[base + g * group + hf * SUBLANES + j]
                r8 = pl.multiple_of((r >> 3) << 3, SUBLANES)
                blk8 = pltpu.roll(xs_ref[pl.ds(r8, SUBLANES), :], (j - r) & (SUBLANES - 1), 0)
                acc = jnp.where(sub == j, blk8, acc)
            halves.append(acc)
        rows = pl.ds(pl.multiple_of(g * group, group), group)
        o_ref[rows, :] = jnp.concatenate(halves, axis=0).astype(BF16)
        return 0

    lax.fori_loop(0, tc // group, body, 0)


def _gather(idx, xs, tc):
    E, cap = idx.shape
    n, D = xs.shape
    tiles = cap // tc
    return pl.pallas_call(
        functools.partial(_gather_body, tc, tiles),
        out_shape=jax.ShapeDtypeStruct((E * cap, D), BF16),
        grid_spec=pltpu.PrefetchScalarGridSpec(
            num_scalar_prefetch=1, grid=(E * tiles,),
            in_specs=[pl.BlockSpec((n, D), lambda i, idx: (0, 0), pipeline_mode=pl.Buffered(1))],
            out_specs=pl.BlockSpec((tc, D), lambda i, idx: (i, 0))),
        compiler_params=_cparams(("arbitrary",)), name="moe_gather",
    )(idx.reshape(-1), xs)


def _ffn_body(xe_ref, wg_ref, wu_ref, wd_ref, o_ref):
    xe = xe_ref[...]
    hg = _dot(xe, wg_ref[0, 0].astype(BF16))
    hu = _dot(xe, wu_ref[0, 0].astype(BF16))
    a = (hg / (1.0 + jnp.exp(-hg))) * hu
    o_ref[...] = _dot(a.astype(BF16), wd_ref[0, 0].astype(BF16))


def _ffn(xe, wg, wu, wd, l, tc):
    _, E, D, FF = wg.shape
    rows = xe.shape[0]
    tiles = rows // E // tc
    wmap = lambda e, t: (l, e, 0, 0)
    row = lambda e, t: (e * tiles + t, 0)
    return pl.pallas_call(
        _ffn_body,
        out_shape=jax.ShapeDtypeStruct((rows, D), F32),
        grid=(E, tiles),
        in_specs=[pl.BlockSpec((tc, D), row), pl.BlockSpec((1, 1, D, FF), wmap),
                  pl.BlockSpec((1, 1, D, FF), wmap), pl.BlockSpec((1, 1, FF, D), wmap)],
        out_specs=pl.BlockSpec((tc, D), row),
        compiler_params=_cparams(("parallel", "parallel")), name="moe_ffn",
    )(xe, wg, wu, wd)


def _combine_body(tc, tiles_per_e, n_scatter, final, idx_ref, gate_ref, ye_ref, x1_ref, mod_ref, fn_ref,
                  o_ref, acc_ref):
    i = pl.program_id(0)

    @pl.when(i == 0)
    def _():
        acc_ref[...] = jnp.zeros_like(acc_ref)

    @pl.when(i < n_scatter)
    def _():
        base = i * tc

        sub = lax.broadcasted_iota(I32, (SUBLANES, LANES), 0)

        nlane = ye_ref.shape[1] // LANES
        ways = 4

        def body(gq, _):
            s0 = pl.multiple_of(gq * (ways * SUBLANES), ways * SUBLANES)
            ye8 = [ye_ref[pl.ds(s0 + w * SUBLANES, SUBLANES), :] for w in range(ways)]
            for j in range(SUBLANES):
                upd = []
                for w in range(ways):
                    r = idx_ref[base + s0 + w * SUBLANES + j]
                    g = gate_ref[base + s0 + w * SUBLANES + j]
                    r8 = pl.multiple_of((r >> 3) << 3, SUBLANES)
                    rolled = pltpu.roll(ye8[w], (r - j) & (SUBLANES - 1), 0)
                    gmask = jnp.where(sub == (r & (SUBLANES - 1)), g, 0.0)
                    upd.append((r8, acc_ref[pl.ds(r8, SUBLANES), :]
                                + rolled * jnp.concatenate([gmask] * nlane, axis=1)))
                for r8, val in upd:
                    acc_ref[pl.ds(r8, SUBLANES), :] = val
            return 0

        lax.fori_loop(0, tc // (ways * SUBLANES), body, 0)

    @pl.when(i >= n_scatter)
    def _():
        tm = x1_ref.shape[0]
        j = i - n_scatter
        rows = pl.ds(pl.multiple_of(j * tm, tm), tm)
        xn = x1_ref[...] + mod_ref[0, 5:6, :] * acc_ref[rows, :]
        o_ref[...] = _rms(xn, fn_ref[...]) if final else xn


def _combine(idx, gate, ye, x1, mod, final_norm, tc, T, final):
    E, cap = idx.shape
    n, D = x1.shape
    tm = ROW_TILE
    tiles = cap // tc
    n_sc = E * tiles
    tpb = T // tm
    nb = mod.shape[0]
    out_row = lambda i, a, b: (jnp.maximum(i - n_sc, 0), 0)
    mod_map = ((lambda i, a, b: (jnp.maximum(i - n_sc, 0) // tpb, 0, 0)) if nb > 1
               else (lambda i, a, b: (0, 0, 0)))
    return pl.pallas_call(
        functools.partial(_combine_body, tc, tiles, n_sc, final),
        out_shape=jax.ShapeDtypeStruct((n, D), F32),
        grid_spec=pltpu.PrefetchScalarGridSpec(
            num_scalar_prefetch=2, grid=(n_sc + n // tm,),
            in_specs=[pl.BlockSpec((tc, D), lambda i, a, b: (jnp.minimum(i, n_sc - 1), 0)),
                      pl.BlockSpec((tm, D), out_row), pl.BlockSpec((1, 6, D), mod_map),
                      pl.BlockSpec((1, D), lambda i, a, b: (0, 0))],
            out_specs=pl.BlockSpec((tm, D), out_row),
            scratch_shapes=[pltpu.VMEM((n, D), F32)]),
        compiler_params=_cparams(("arbitrary",)), name="moe_combine",
    )(idx.reshape(-1), gate.reshape(-1), ye, x1, mod, final_norm)


def _rope_tables(T):
    t = jnp.arange(T)
    row = (t // GRID_W).astype(F32)
    col = (t % GRID_W).astype(F32)

    def table(head_w, lane_off, rep):
        half = head_w // 4
        freqs = ROPE_BASE ** (-jnp.arange(half, dtype=F32) / half)
        blk = []
        for pos in (row, col):
            ang = pos[:, None] * freqs[None, :]
            blk.append((jnp.cos(ang), jnp.sin(ang)))
        cos = jnp.concatenate([blk[0][0], blk[0][0], blk[1][0], blk[1][0]], axis=1)
        sin = jnp.concatenate([-blk[0][1], blk[0][1], -blk[1][1], blk[1][1]], axis=1)
        cos = jnp.concatenate([cos] * rep, axis=1)
        sin = jnp.concatenate([sin] * rep, axis=1)
        pad_r = LANES - lane_off - cos.shape[1]
        cos = jnp.pad(cos, ((0, 0), (lane_off, pad_r)), constant_values=1.0)
        sin = jnp.pad(sin, ((0, 0), (lane_off, pad_r)))
        return cos, sin

    c64, s64 = table(GQA_DIM, 0, 2)
    cm, sm = table(MLA_ROPE, MLA_NOPE, 1)
    return c64, s64, cm, sm


def _layer_weights(l, w_in, norm1, norm2, mla_q_norm, mla_kv_norm, mla_w_uq, mla_w_uk, mla_w_uv,
                   gqa_q_norm, gqa_k_norm, w_br_mla, w_br_gqa, w_br_diff, w_out, w_router):
    D = D_MODEL
    wi = w_in[l]
    seg = lambda a, b: wi[:, a:b]
    w1 = jnp.concatenate([seg(0, 640), seg(672, 2976), seg(640, 672),
                          jnp.zeros((D, _W1_COLS - _O_KPE - MLA_ROPE), F32)], axis=1).astype(BF16)
    uq = mla_w_uq[l].reshape(MLA_Q_RANK, MLA_HEADS, MLA_NOPE + MLA_ROPE)
    wuq = jnp.pad(uq, ((0, 0), (0, 0), (0, HEAD_PAD - MLA_NOPE - MLA_ROPE))).reshape(MLA_Q_RANK, -1)
    wuk = jnp.pad(mla_w_uk[l], ((0, 0), (0, 0), (0, HEAD_PAD - MLA_NOPE))).reshape(MLA_KV_RANK, -1)
    wuv = mla_w_uv[l].reshape(MLA_KV_RANK, MLA_HEADS * MLA_V)
    grp = jnp.arange(512) // GQA_DIM
    bd = (grp[:, None] == grp[None, :]).astype(BF16)
    wr = jnp.pad(w_router[l], ((0, 0), (0, LANES - N_EXPERTS)))
    wr_hi = wr.astype(BF16)
    wr_lo = (wr - wr_hi.astype(F32)).astype(BF16)
    return {
        "norm1": norm1[l][None], "norm2": norm2[l][None], "w1": w1,
        "mla_q_norm": mla_q_norm[l][None], "mla_kv_norm": mla_kv_norm[l][None],
        "gqa_q_norm": jnp.tile(gqa_q_norm[l], GQA_HEADS)[None],
        "gqa_k_norm": jnp.tile(gqa_k_norm[l], GQA_KV_HEADS)[None],
        "wuq": wuq.astype(BF16), "wuk": wuk.astype(BF16), "wuv": wuv.astype(BF16), "bd": bd,
        "wgate": wi[:, 2976:].astype(BF16), "wbr_mla": w_br_mla[l].astype(BF16),
        "wbr_gqa": w_br_gqa[l].astype(BF16), "wbr_diff": w_br_diff[l].astype(BF16),
        "w_out": w_out[l].astype(BF16), "wr_hi": wr_hi, "wr_lo": wr_lo,
    }


def _trunk(x, mod, lw, ew, lam, lam_init, tabs, ctx, B, T, final_norm, final, own):
    n = B * T
    outs = _in_proj(x, mod, lw, tabs, T, own)
    qm, km, vm, gq, gk, gv, dq, dk, dv = [a.reshape(B, T, -1) for a in outs[:9]]
    if ctx is None:
        pm, pg, pd = [(km, vm)], [(gk, gv)], [(dk, dv)]
    else:
        ckm, cvm, cgk, cgv, cdk, cdv = ctx
        pm, pg, pd = [(ckm, cvm), (km, vm)], [(cgk, cgv), (gk, gv)], [(cdk, cdv), (dk, dv)]
    om = _attention("mla", qm, pm, []).reshape(n, -1)
    og = _attention("gqa", gq, pg, []).reshape(n, -1)
    od = _attention("diff", dq, pd, list(lam), lam_init).reshape(n, -1)
    x1, h2, aff = _out_proj(x, mod, om, og, od, lw, T)
    cap = CAPACITY_FACTOR * n // N_EXPERTS
    idx, gate = _route(aff, cap)
    tc = min(MOE_ROW_TILE, cap)
    xe = _gather(idx, h2, tc)
    ye = _ffn(xe, ew[0], ew[1], ew[2], ew[3], min(FFN_TILE, cap))
    xn = _combine(idx, gate, ye, x1, mod, final_norm, tc, T, final)
    return xn, outs[9:]


def kernel(x_prompt, x_sample, cache_mla_ckv, cache_mla_kpe, cache_gqa_k, cache_gqa_v, cache_diff_k, cache_diff_v, c, c_ctx, w_ada, b_ada, norm1, norm2, w_in, mla_q_norm, mla_kv_norm, mla_w_uq, mla_w_uk, mla_w_uv, gqa_q_norm, gqa_k_norm, diff_lam_q1, diff_lam_k1, diff_lam_q2, diff_lam_k2, diff_subln, w_br_mla, w_br_gqa, w_br_diff, w_out, w_router, w_gate, w_up, w_down, final_norm):
    Bp, Tp, D = x_prompt.shape
    Bs, Ts, _ = x_sample.shape
    L = w_in.shape[0]
    P = cache_mla_ckv.shape[2]
    assert D == D_MODEL and Tp % ROW_TILE == 0 and Ts % ROW_TILE == 0 and Ts % GRID_W == 0

    nrow = -(-(1 + Bs) // 8) * 8
    cond = jnp.zeros((nrow, D), F32).at[0].set(c_ctx).at[1:1 + Bs].set(c)
    mod_all = _ada(cond, w_ada, b_ada).reshape(L, nrow, 6, D)

    tabs = _rope_tables(Ts)
    fn = final_norm[None]
    xp = x_prompt.reshape(Bp * Tp, D)
    xs = x_sample.reshape(Bs * Ts, D)
    owns = []
    for l in range(L):
        lw = _layer_weights(l, w_in, norm1, norm2, mla_q_norm, mla_kv_norm, mla_w_uq, mla_w_uk, mla_w_uv,
                            gqa_q_norm, gqa_k_norm, w_br_mla, w_br_gqa, w_br_diff, w_out, w_router)
        ew = (w_gate, w_up, w_down, l)
        lam = (diff_lam_q1[l][None], diff_lam_k1[l][None], diff_lam_q2[l][None], diff_lam_k2[l][None],
               diff_subln[l][None])
        lam_init = 0.8 - 0.6 * math.exp(-0.3 * l)
        final = l == L - 1
        xp, own = _trunk(xp, mod_all[l, 0:1], lw, ew, lam, lam_init, None, None, Bp, Tp, fn, final, True)
        owns.append(own)
        kpe_pad = jnp.pad(cache_mla_kpe[:, l].reshape(Bs * P, MLA_ROPE), ((0, 0), (0, LANES - MLA_ROPE)))
        ckm, cvm = _cache_proj(cache_mla_ckv[:, l].reshape(Bs * P, MLA_KV_RANK), kpe_pad, lw)
        ctx = (ckm.reshape(Bs, P, -1), cvm.reshape(Bs, P, -1),
               cache_gqa_k[:, l].reshape(Bs, P, -1), cache_gqa_v[:, l].reshape(Bs, P, -1),
               cache_diff_k[:, l].reshape(Bs, P, -1), cache_diff_v[:, l].reshape(Bs, P, -1))
        xs, _ = _trunk(xs, mod_all[l, 1:1 + Bs], lw, ew, lam, lam_init, tabs, ctx, Bs, Ts, fn, final, False)

    def stack(k, shape):
        return jnp.stack([o[k].reshape((Bp, Tp) + shape) for o in owns], axis=1)

    return (xp.reshape(Bp, Tp, D), xs.reshape(Bs, Ts, D),
            stack(0, (MLA_KV_RANK,)), stack(1, (MLA_ROPE,)),
            stack(2, (GQA_KV_HEADS, GQA_DIM)), stack(3, (GQA_KV_HEADS, GQA_DIM)),
            stack(4, (DIFF_HEADS, 2 * DIFF_DIM)), stack(5, (DIFF_HEADS, DIFF_V)))
```

```python
import functools
import math

import jax
import jax.numpy as jnp
from jax import lax
from jax.experimental import pallas as pl
from jax.experimental.pallas import tpu as pltpu

F32 = jnp.float32
BF16 = jnp.bfloat16
I32 = jnp.int32

D_MODEL = 1024
GRID_W = 64
ROPE_BASE = 10000.0
EPS = 1e-6
MLA_HEADS, MLA_NOPE, MLA_ROPE, MLA_V = 8, 64, 32, 64
MLA_Q_RANK, MLA_KV_RANK = 384, 256
GQA_HEADS, GQA_KV_HEADS, GQA_DIM = 8, 2, 64
DIFF_HEADS, DIFF_DIM = 4, 64
DIFF_V = 2 * DIFF_DIM
N_EXPERTS, EXPERT_FF, CAPACITY_FACTOR = 16, 1024, 2

LANES = 128
SUBLANES = 8
HEAD_PAD = 128
ROW_TILE = 512
OUT_ROW_TILE = 256
Q_TILE = 512
ATTN_ROWS_PER_STEP = 1024
FFN_TILE = 1024
MOE_ROW_TILE = 512
VMEM_LIMIT = 56 * 1024 * 1024

_O_CQ, _O_CKV, _O_GQ, _O_GK, _O_GV, _O_DQ, _O_DK, _O_DV, _O_KPE, _W1_COLS = (
    0, 384, 640, 1152, 1280, 1408, 1920, 2432, 2944, 3072)
LOG2E = 1.4426950408889634


def _cparams(sem):
    return pltpu.CompilerParams(dimension_semantics=sem, vmem_limit_bytes=VMEM_LIMIT)


def _dot(a, b):
    return jnp.dot(a, b, preferred_element_type=F32)


def _dot_nt(a, b):
    return lax.dot_general(a, b, (((1,), (1,)), ((), ())), preferred_element_type=F32)


def _rms(x, g):
    return x * lax.rsqrt(jnp.mean(x * x, axis=-1, keepdims=True) + EPS) * g


def _lane_iota(shape):
    return lax.broadcasted_iota(I32, shape, len(shape) - 1)


def _rope(x, cos, sin, width):
    outs = []
    even = ((_lane_iota((1, LANES)) // width) % 2) == 0
    for j in range(x.shape[1] // LANES):
        xb = x[:, j * LANES:(j + 1) * LANES]
        partner = jnp.where(even, pltpu.roll(xb, LANES - width, 1), pltpu.roll(xb, width, 1))
        outs.append(xb * cos + partner * sin)
    return outs[0] if len(outs) == 1 else jnp.concatenate(outs, axis=1)


def _group_mean_sq(x, bd):
    sq = x * x
    hi = sq.astype(BF16)
    lo = (sq - hi.astype(F32)).astype(BF16)
    return (_dot(hi, bd) + _dot(lo, bd)) * (1.0 / GQA_DIM)


def _ada_body(c_ref, w_ref, b_ref, o_ref):
    c = c_ref[...]
    s = c / (1.0 + jnp.exp(-c))
    o_ref[0] = _dot(s.astype(BF16), w_ref[0].astype(BF16)) + b_ref[0]


def _ada(cond, w_ada, b_ada):
    L, D, N = w_ada.shape
    R = cond.shape[0]
    tn = 512
    return pl.pallas_call(
        _ada_body,
        out_shape=jax.ShapeDtypeStruct((L, R, N), F32),
        grid=(L, N // tn),
        in_specs=[pl.BlockSpec((R, D), lambda l, j: (0, 0)),
                  pl.BlockSpec((1, D, tn), lambda l, j: (l, 0, j)),
                  pl.BlockSpec((1, 1, tn), lambda l, j: (l, 0, j))],
        out_specs=pl.BlockSpec((1, R, tn), lambda l, j: (l, 0, j)),
        compiler_params=_cparams(("parallel", "parallel")),
        name="ada",
    )(cond, w_ada, b_ada.reshape(L, 1, N))


def _in_body(rope, own, *refs):
    (x_ref, mod_ref, n1_ref, w_ref, qn_ref, kvn_ref, gqn_ref, gkn_ref, wuq_ref, wuk_ref, wuv_ref,
     bd_ref) = refs[:12]
    k = 12
    if rope:
        c64_ref, s64_ref, cm_ref, sm_ref = refs[k:k + 4]
        k += 4
    (qm_ref, km_ref, vm_ref, gq_ref, gk_ref, gv_ref, dq_ref, dk_ref, dv_ref) = refs[k:k + 9]
    k += 9
    if own:
        ockv_ref, okpe_ref, ogk_ref, ogv_ref, odk_ref, odv_ref = refs[k:k + 6]

    x = x_ref[...]
    shift = mod_ref[0, 0:1, :]
    scale = mod_ref[0, 1:2, :]
    h = _rms(x, n1_ref[...]) * (1.0 + scale) + shift
    z = _dot(h.astype(BF16), w_ref[...])
    bd = bd_ref[...]

    def rot64(u):
        return _rope(u, c64_ref[...], s64_ref[...], 16) if rope else u

    def rotm(u):
        return _rope(u, cm_ref[...], sm_ref[...], 8) if rope else u

    cq = _rms(z[:, _O_CQ:_O_CKV], qn_ref[...])
    q = _dot(cq.astype(BF16), wuq_ref[...])
    qm_ref[...] = (rotm(q) * (LOG2E * (MLA_NOPE + MLA_ROPE) ** -0.5)).astype(BF16)
    ckv = _rms(z[:, _O_CKV:_O_GQ], kvn_ref[...])
    kpe = z[:, _O_KPE:_W1_COLS]
    ckv_b = ckv.astype(BF16)
    kpe_at = pltpu.roll(kpe, MLA_NOPE, 1)
    knope = _dot(ckv_b, wuk_ref[...])
    kfull = knope + jnp.concatenate([kpe_at] * MLA_HEADS, axis=1)
    km_ref[...] = rotm(kfull).astype(BF16)
    vm_ref[...] = _dot(ckv_b, wuv_ref[...]).astype(BF16)
    gq = z[:, _O_GQ:_O_GK]
    gq = gq * lax.rsqrt(_group_mean_sq(gq, bd) + EPS) * gqn_ref[...]
    gq_ref[...] = (rot64(gq) * (LOG2E * GQA_DIM ** -0.5)).astype(BF16)
    gk = z[:, _O_GK:_O_GV]
    gk = gk * lax.rsqrt(_group_mean_sq(gk, bd[:LANES, :LANES]) + EPS) * gkn_ref[...]
    gk_ref[...] = rot64(gk).astype(BF16)
    gv = z[:, _O_GV:_O_DQ]
    gv_ref[...] = gv.astype(BF16)
    dq_ref[...] = (rot64(z[:, _O_DQ:_O_DK]) * (LOG2E * DIFF_DIM ** -0.5)).astype(BF16)
    dk = z[:, _O_DK:_O_DV]
    dk_ref[...] = rot64(dk).astype(BF16)
    dv = z[:, _O_DV:_O_KPE]
    dv_ref[...] = dv.astype(BF16)
    if own:
        ockv_ref[...] = ckv
        okpe_ref[...] = kpe[:, :MLA_ROPE]
        ogk_ref[...] = gk
        ogv_ref[...] = gv
        odk_ref[...] = dk
        odv_ref[...] = dv


def _layer_spec(a, l):
    zeros = (0,) * (a.ndim - 1)
    return pl.BlockSpec((None,) + a.shape[1:], lambda *_: (l,) + zeros)


def _mod_spec(mod_all, msel, l, tpb, tile_of=lambda i: i):
    row0, per_batch = msel
    D = mod_all.shape[-1]
    if per_batch:
        return pl.BlockSpec((None, 1, 6, D), lambda i, *_: (l, row0 + tile_of(i) // tpb, 0, 0))
    return pl.BlockSpec((None, 1, 6, D), lambda i, *_: (l, row0, 0, 0))


def _row_tile(n, T, msel, tile=ROW_TILE):
    return min(tile, T) if msel[1] else min(tile, n)


def _in_proj(x, mod_all, msel, W, l, tabs, T, own):
    n, D = x.shape
    tm = _row_tile(n, T, msel)
    rope = tabs is not None
    tpb = max(T // tm, 1)
    row = lambda i: (i, 0)
    stacked = [W[k] for k in ("norm1", "w1", "mla_q_norm", "mla_kv_norm", "gqa_q_norm", "gqa_k_norm",
                              "wuq", "wuk", "wuv")]
    ins = [x, mod_all] + stacked + [W["bd"]]
    in_specs = ([pl.BlockSpec((tm, D), row), _mod_spec(mod_all, msel, l, tpb)]
                + [_layer_spec(a, l) for a in stacked] + [pl.BlockSpec(W["bd"].shape, lambda i: (0, 0))])
    if rope:
        ins += list(tabs)
        in_specs += [pl.BlockSpec((tm, LANES), lambda i: (i % tpb, 0))] * 4
    widths = [(1024, BF16), (1024, BF16), (512, BF16), (512, BF16), (128, BF16), (128, BF16),
              (512, BF16), (512, BF16), (512, BF16)]
    if own:
        widths += [(256, F32), (32, F32), (128, F32), (128, F32), (512, F32), (512, F32)]
    out_shape = [jax.ShapeDtypeStruct((n, w), dt) for w, dt in widths]
    out_specs = [pl.BlockSpec((tm, w), row) for w, _ in widths]
    return pl.pallas_call(
        functools.partial(_in_body, rope, own),
        out_shape=out_shape, grid=(n // tm,), in_specs=in_specs, out_specs=out_specs,
        compiler_params=_cparams(("parallel",)), name="in_proj",
    )(*ins)


def _cache_body(ckv_ref, kpe_ref, wuk_ref, wuv_ref, km_ref, vm_ref):
    ckv_b = ckv_ref[...].astype(BF16)
    kpe_at = pltpu.roll(kpe_ref[...], MLA_NOPE, 1)
    km_ref[...] = (_dot(ckv_b, wuk_ref[...]) + jnp.concatenate([kpe_at] * MLA_HEADS, axis=1)).astype(BF16)
    vm_ref[...] = _dot(ckv_b, wuv_ref[...]).astype(BF16)


def _cache_proj(ckv, kpe_pad, W, l):
    n = ckv.shape[0]
    tm = min(ROW_TILE, n)
    row = lambda i: (i, 0)
    return pl.pallas_call(
        _cache_body,
        out_shape=[jax.ShapeDtypeStruct((n, 1024), BF16), jax.ShapeDtypeStruct((n, 512), BF16)],
        grid=(n // tm,),
        in_specs=[pl.BlockSpec((tm, MLA_KV_RANK), row), pl.BlockSpec((tm, LANES), row),
                  _layer_spec(W["wuk"], l), _layer_spec(W["wuv"], l)],
        out_specs=[pl.BlockSpec((tm, 1024), row), pl.BlockSpec((tm, 512), row)],
        compiler_params=_cparams(("parallel",)), name="cache_proj",
    )(ckv, kpe_pad, W["wuk"], W["wuv"])


def _swap_halves(a):
    return pltpu.roll(a.astype(F32), LANES // 2, 1).astype(a.dtype)


def _softmax_av(q, ks, vs, v_has_ones):
    ss = [_dot_nt(q, k) for k in ks]
    mx = ss[0].max(axis=-1, keepdims=True)
    for s in ss[1:]:
        mx = jnp.maximum(mx, s.max(axis=-1, keepdims=True))
    acc, den = None, None
    for s, v in zip(ss, vs):
        p = jnp.exp2(s - mx)
        if not v_has_ones:
            d = p.sum(axis=-1, keepdims=True)
            den = d if den is None else den + d
        a = _dot(p.astype(BF16), v)
        acc = a if acc is None else acc + a
    return acc, den


def _attn_body(mode, nparts, lam_init, *refs):
    q_ref = refs[0]
    kv = [(refs[1 + 2 * i], refs[2 + 2 * i]) for i in range(nparts)]
    k = 1 + 2 * nparts
    if mode == "diff":
        lq1, lk1, lq2, lk2, sub_ref = refs[k:k + 5]
        k += 5
    o_ref = refs[k]

    lo = _lane_iota((1, LANES)) < (LANES // 2)
    hi = jnp.logical_not(lo)
    ones = jnp.ones((1, LANES), BF16)
    cols = lambda j: slice(j * LANES, (j + 1) * LANES)

    def normalise(full):
        return full / pltpu.roll(full, LANES // 2, 1)

    if mode == "diff":
        lam = (jnp.exp(jnp.sum(lq1[...] * lk1[...], axis=-1, keepdims=True))
               - jnp.exp(jnp.sum(lq2[...] * lk2[...], axis=-1, keepdims=True)) + lam_init)
    for bi in range(q_ref.shape[0]):
        if mode == "gqa":
            ks = [kr[bi].astype(BF16) for kr, _ in kv]
            vs = [vr[bi].astype(BF16) for _, vr in kv]
            ks_sw = [_swap_halves(a) for a in ks]
            vs_sw = [_swap_halves(a) for a in vs]
            for pair in range(GQA_HEADS // 2):
                qp = q_ref[bi, :, cols(pair)]
                res = []
                for L in range(2):
                    g = (2 * pair + L) // (GQA_HEADS // GQA_KV_HEADS)
                    mine = lo if L == 0 else hi
                    kk = ks if g == L else ks_sw
                    vv = [jnp.where(mine, a, ones) for a in (vs if g == L else vs_sw)]
                    full, _ = _softmax_av(jnp.where(mine, qp, jnp.zeros_like(qp)), kk, vv, True)
                    res.append(normalise(full))
                o_ref[bi, :, cols(pair)] = jnp.where(lo, res[0], res[1]).astype(o_ref.dtype)
        elif mode == "mla":
            for pair in range(MLA_HEADS // 2):
                vs = [vr[bi, :, cols(pair)] for _, vr in kv]
                res = []
                for L in range(2):
                    mine = lo if L == 0 else hi
                    kk = [kr[bi, :, cols(2 * pair + L)] for kr, _ in kv]
                    vv = [jnp.where(mine, a, ones) for a in vs]
                    full, _ = _softmax_av(q_ref[bi, :, cols(2 * pair + L)], kk, vv, True)
                    res.append(normalise(full))
                o_ref[bi, :, cols(pair)] = jnp.where(lo, res[0], res[1]).astype(o_ref.dtype)
        else:
            for h in range(DIFF_HEADS):
                qh = q_ref[bi, :, cols(h)]
                kk = [kr[bi, :, cols(h)].astype(BF16) for kr, _ in kv]
                vv = [vr[bi, :, cols(h)].astype(BF16) for _, vr in kv]
                zero = jnp.zeros_like(qh)
                a1, d1 = _softmax_av(jnp.where(lo, qh, zero), kk, vv, False)
                a2, d2 = _softmax_av(jnp.where(lo, zero, qh), kk, vv, False)
                o = _rms(a1 / d1 - lam * (a2 / d2), sub_ref[...]) * (1.0 - lam_init)
                o_ref[bi, :, cols(h)] = o.astype(o_ref.dtype)


def _attention(mode, q, parts, extra, l=0, lam_init=0.0):
    B, T, C = q.shape
    tq = min(Q_TILE, T)
    bb = max(1, min(B, ATTN_ROWS_PER_STEP // tq)) if (T == tq and mode == "diff") else 1
    while B % bb:
        bb -= 1
    ins = [q]
    in_specs = [pl.BlockSpec((bb, tq, C), lambda b, i: (b, i, 0))]
    for k, v in parts:
        ins += [k, v]
        in_specs += [pl.BlockSpec((bb,) + k.shape[1:], lambda b, i: (b, 0, 0)),
                     pl.BlockSpec((bb,) + v.shape[1:], lambda b, i: (b, 0, 0))]
    for e in extra:
        ins.append(e)
        in_specs.append(_layer_spec(e, l))
    return pl.pallas_call(
        functools.partial(_attn_body, mode, len(parts), lam_init),
        out_shape=jax.ShapeDtypeStruct((B, T, 512), BF16),
        grid=(B // bb, T // tq),
        in_specs=in_specs,
        out_specs=pl.BlockSpec((bb, tq, 512), lambda b, i: (b, i, 0)),
        compiler_params=_cparams(("parallel", "parallel")), name="attn_" + mode,
    )(*ins)


def _out_body(x_ref, mod_ref, n1_ref, n2_ref, om_ref, og_ref, od_ref, wg_ref, bm_ref, bg_ref, bdf_ref,
              wo_ref, rh_ref, rl_ref, x1_ref, h2_ref, aff_ref):
    x = x_ref[...]
    shift1, scale1, gate1 = mod_ref[0, 0:1, :], mod_ref[0, 1:2, :], mod_ref[0, 2:3, :]
    shift2, scale2 = mod_ref[0, 3:4, :], mod_ref[0, 4:5, :]
    h = (_rms(x, n1_ref[...]) * (1.0 + scale1) + shift1).astype(BF16)
    g = _dot(h, wg_ref[...])
    D = x.shape[1]
    sig = lambda u: 1.0 / (1.0 + jnp.exp(-u))
    merged = (sig(g[:, :D]) * _dot(om_ref[...], bm_ref[...])
              + sig(g[:, D:2 * D]) * _dot(og_ref[...], bg_ref[...])
              + sig(g[:, 2 * D:]) * _dot(od_ref[...], bdf_ref[...]))
    y = _dot(merged.astype(BF16), wo_ref[...])
    x1 = x + gate1 * y
    x1_ref[...] = x1
    h2 = _rms(x1, n2_ref[...]) * (1.0 + scale2) + shift2
    h2_ref[...] = h2
    hi = h2.astype(BF16)
    lo = (h2 - hi.astype(F32)).astype(BF16)
    logits = _dot(hi, rh_ref[...]) + _dot(lo, rh_ref[...]) + _dot(hi, rl_ref[...])
    valid = _lane_iota((1, LANES)) < N_EXPERTS
    logits = jnp.where(valid, logits, -1e30)
    e = jnp.exp(logits - logits.max(axis=-1, keepdims=True))
    aff = e / e.sum(axis=-1, keepdims=True)
    aff_ref[...] = aff.T[:N_EXPERTS, :]


def _out_proj(x, mod_all, msel, om, og, od, W, l, T):
    n, D = x.shape
    tm = _row_tile(n, T, msel, OUT_ROW_TILE)
    tpb = max(T // tm, 1)
    row = lambda i: (i, 0)
    ws = [W[k] for k in ("wgate", "wbr_mla", "wbr_gqa", "wbr_diff", "w_out", "wr_hi", "wr_lo")]
    return pl.pallas_call(
        _out_body,
        out_shape=[jax.ShapeDtypeStruct((n, D), F32), jax.ShapeDtypeStruct((n, D), F32),
                   jax.ShapeDtypeStruct((N_EXPERTS, n), F32)],
        grid=(n // tm,),
        in_specs=[pl.BlockSpec((tm, D), row), _mod_spec(mod_all, msel, l, tpb), _layer_spec(W["norm1"], l),
                  _layer_spec(W["norm2"], l), pl.BlockSpec((tm, 512), row), pl.BlockSpec((tm, 512), row),
                  pl.BlockSpec((tm, 512), row)] + [_layer_spec(w, l) for w in ws],
        out_specs=[pl.BlockSpec((tm, D), row), pl.BlockSpec((tm, D), row),
                   pl.BlockSpec((N_EXPERTS, tm), lambda i: (0, i))],
        compiler_params=_cparams(("parallel",)), name="out_proj",
    )(x, mod_all, W["norm1"], W["norm2"], om, og, od, *ws)


def _route_body(cap, aff_ref, idx_ref, gate_ref):
    v = aff_ref[...]
    E, n = v.shape
    nbits = max(1, (n - 1).bit_length())
    lane = _lane_iota((E, n))

    def count(mask):
        return jnp.sum(mask.astype(I32), axis=1, keepdims=True)

    def step(i, t):
        cand = t | lax.shift_left(jnp.int32(1), 30 - i)
        return jnp.where(count(v >= lax.bitcast_convert_type(cand, F32)) >= cap, cand, t)

    thr = lax.bitcast_convert_type(lax.fori_loop(0, 31, step, jnp.zeros((E, 1), I32)), F32)

    def prefix(u):
        for b in range(nbits):
            s = 1 << b
            u = u + jnp.where(lane >= s, pltpu.roll(u, s, 1), 0)
        return u

    gt = v > thr
    eq = v == thr
    need = cap - count(gt)
    eq_i = eq.astype(I32)
    eq_rank = prefix(eq_i) - eq_i
    sel = jnp.where(gt | (eq & (eq_rank < need)), 1, 0).astype(I32)
    dist = lane + 1 - prefix(sel)
    tok = lane
    for b in range(nbits):
        s = 1 << b
        mv = sel * ((dist >> b) & 1)
        stay = sel - mv
        take = pltpu.roll(mv, n - s, 1) == 1
        tok = jnp.where(take, pltpu.roll(tok, n - s, 1), tok)
        v = jnp.where(take, pltpu.roll(v, n - s, 1), v)
        dist = jnp.where(take, pltpu.roll(dist, n - s, 1), dist)
        sel = jnp.where(take, 1, stay)
    idx_ref[...] = tok[:, :cap]
    gate_ref[...] = v[:, :cap]


def _route(aff_t, cap):
    E, n = aff_t.shape
    return pl.pallas_call(
        functools.partial(_route_body, cap),
        out_shape=[jax.ShapeDtypeStruct((E, cap), I32), jax.ShapeDtypeStruct((E, cap), F32)],
        compiler_params=pltpu.CompilerParams(vmem_limit_bytes=VMEM_LIMIT), name="route",
    )(aff_t)


def _gather_body(tc, tiles_per_e, idx_ref, xs_ref, o_ref):
    i = pl.program_id(0)
    base = i * tc

    group = 16
    D = xs_ref.shape[1]
    sub = lax.broadcasted_iota(I32, (SUBLANES, D), 0)

    def body(g, _):
        halves = []
        for hf in range(group // SUBLANES):
            acc = jnp.zeros((SUBLANES, D), F32)
            for j in range(SUBLANES):
                r = idx_ref[base + g * group + hf * SUBLANES + j]
                r8 = pl.multiple_of((r >> 3) << 3, SUBLANES)
                blk8 = pltpu.roll(xs_ref[pl.ds(r8, SUBLANES), :], (j - r) & (SUBLANES - 1), 0)
                acc = jnp.where(sub == j, blk8, acc)
            halves.append(acc)
        rows = pl.ds(pl.multiple_of(g * group, group), group)
        o_ref[rows, :] = jnp.concatenate(halves, axis=0).astype(BF16)
        return 0

    lax.fori_loop(0, tc // group, body, 0)


def _gather(idx, xs, tc):
    E, cap = idx.shape
    n, D = xs.shape
    tiles = cap // tc
    return pl.pallas_call(
        functools.partial(_gather_body, tc, tiles),
        out_shape=jax.ShapeDtypeStruct((E * cap, D), BF16),
        grid_spec=pltpu.PrefetchScalarGridSpec(
            num_scalar_prefetch=1, grid=(E * tiles,),
            in_specs=[pl.BlockSpec((n, D), lambda i, idx: (0, 0), pipeline_mode=pl.Buffered(1))],
            out_specs=pl.BlockSpec((tc, D), lambda i, idx: (i, 0))),
        compiler_params=_cparams(("arbitrary",)), name="moe_gather",
    )(idx.reshape(-1), xs)


def _ffn_body(xe_ref, wg_ref, wu_ref, wd_ref, o_ref):
    xe = xe_ref[...]
    hg = _dot(xe, wg_ref[0, 0].astype(BF16))
    hu = _dot(xe, wu_ref[0, 0].astype(BF16))
    a = (hg / (1.0 + jnp.exp(-hg))) * hu
    o_ref[...] = _dot(a.astype(BF16), wd_ref[0, 0].astype(BF16))


def _ffn(xe, wg, wu, wd, l, tc):
    _, E, D, FF = wg.shape
    rows = xe.shape[0]
    tiles = rows // E // tc
    wmap = lambda e, t: (l, e, 0, 0)
    row = lambda e, t: (e * tiles + t, 0)
    return pl.pallas_call(
        _ffn_body,
        out_shape=jax.ShapeDtypeStruct((rows, D), F32),
        grid=(E, tiles),
        in_specs=[pl.BlockSpec((tc, D), row), pl.BlockSpec((1, 1, D, FF), wmap),
                  pl.BlockSpec((1, 1, D, FF), wmap), pl.BlockSpec((1, 1, FF, D), wmap)],
        out_specs=pl.BlockSpec((tc, D), row),
        compiler_params=_cparams(("parallel", "parallel")), name="moe_ffn",
    )(xe, wg, wu, wd)


def _combine_body(tc, tiles_per_e, n_scatter, final, idx_ref, gate_ref, ye_ref, x1_ref, mod_ref, fn_ref,
                  o_ref, acc_ref):
    i = pl.program_id(0)

    @pl.when(i == 0)
    def _():
        acc_ref[...] = jnp.zeros_like(acc_ref)

    @pl.when(i < n_scatter)
    def _():
        base = i * tc

        sub = lax.broadcasted_iota(I32, (SUBLANES, LANES), 0)

        nlane = ye_ref.shape[1] // LANES
        ways = 4

        def body(gq, _):
            s0 = pl.multiple_of(gq * (ways * SUBLANES), ways * SUBLANES)
            ye8 = [ye_ref[pl.ds(s0 + w * SUBLANES, SUBLANES), :] for w in range(ways)]
            for j in range(SUBLANES):
                upd = []
                for w in range(ways):
                    r = idx_ref[base + s0 + w * SUBLANES + j]
                    g = gate_ref[base + s0 + w * SUBLANES + j]
                    r8 = pl.multiple_of((r >> 3) << 3, SUBLANES)
                    rolled = pltpu.roll(ye8[w], (r - j) & (SUBLANES - 1), 0)
                    gmask = jnp.where(sub == (r & (SUBLANES - 1)), g, 0.0)
                    upd.append((r8, acc_ref[pl.ds(r8, SUBLANES), :]
                                + rolled * jnp.concatenate([gmask] * nlane, axis=1)))
                for r8, val in upd:
                    acc_ref[pl.ds(r8, SUBLANES), :] = val
            return 0

        lax.fori_loop(0, tc // (ways * SUBLANES), body, 0)

    @pl.when(i >= n_scatter)
    def _():
        tm = x1_ref.shape[0]
        j = i - n_scatter
        rows = pl.ds(pl.multiple_of(j * tm, tm), tm)
        xn = x1_ref[...] + mod_ref[0, 5:6, :] * acc_ref[rows, :]
        o_ref[...] = _rms(xn, fn_ref[...]) if final else xn


def _combine(idx, gate, ye, x1, mod_all, msel, l, final_norm, tc, T, final):
    E, cap = idx.shape
    n, D = x1.shape
    tm = _row_tile(n, T, msel)
    tiles = cap // tc
    n_sc = E * tiles
    tpb = max(T // tm, 1)
    out_tile = lambda i: jnp.maximum(i - n_sc, 0)
    out_row = lambda i, a, b: (out_tile(i), 0)
    return pl.pallas_call(
        functools.partial(_combine_body, tc, tiles, n_sc, final),
        out_shape=jax.ShapeDtypeStruct((n, D), F32),
        grid_spec=pltpu.PrefetchScalarGridSpec(
            num_scalar_prefetch=2, grid=(n_sc + n // tm,),
            in_specs=[pl.BlockSpec((tc, D), lambda i, a, b: (jnp.minimum(i, n_sc - 1), 0)),
                      pl.BlockSpec((tm, D), out_row), _mod_spec(mod_all, msel, l, tpb, out_tile),
                      pl.BlockSpec((1, D), lambda i, a, b: (0, 0))],
            out_specs=pl.BlockSpec((tm, D), out_row),
            scratch_shapes=[pltpu.VMEM((n, D), F32)]),
        compiler_params=_cparams(("arbitrary",)), name="moe_combine",
    )(idx.reshape(-1), gate.reshape(-1), ye, x1, mod_all, final_norm)


def _rope_tables(T):
    t = jnp.arange(T)
    row = (t // GRID_W).astype(F32)
    col = (t % GRID_W).astype(F32)

    def table(head_w, lane_off, rep):
        half = head_w // 4
        freqs = ROPE_BASE ** (-jnp.arange(half, dtype=F32) / half)
        blk = []
        for pos in (row, col):
            ang = pos[:, None] * freqs[None, :]
            blk.append((jnp.cos(ang), jnp.sin(ang)))
        cos = jnp.concatenate([blk[0][0], blk[0][0], blk[1][0], blk[1][0]], axis=1)
        sin = jnp.concatenate([-blk[0][1], blk[0][1], -blk[1][1], blk[1][1]], axis=1)
        cos = jnp.concatenate([cos] * rep, axis=1)
        sin = jnp.concatenate([sin] * rep, axis=1)
        pad_r = LANES - lane_off - cos.shape[1]
        cos = jnp.pad(cos, ((0, 0), (lane_off, pad_r)), constant_values=1.0)
        sin = jnp.pad(sin, ((0, 0), (lane_off, pad_r)))
        return cos, sin

    c64, s64 = table(GQA_DIM, 0, 2)
    cm, sm = table(MLA_ROPE, MLA_NOPE, 1)
    return c64, s64, cm, sm


def _prep_weights(w_in, norm1, norm2, mla_q_norm, mla_kv_norm, mla_w_uq, mla_w_uk, mla_w_uv,
                  gqa_q_norm, gqa_k_norm, w_br_mla, w_br_gqa, w_br_diff, w_out, w_router):
    L, D = w_in.shape[0], D_MODEL
    seg = lambda a, b: w_in[:, :, a:b]
    w1 = jnp.concatenate([seg(0, 640), seg(672, 2976), seg(640, 672),
                          jnp.zeros((L, D, _W1_COLS - _O_KPE - MLA_ROPE), F32)], axis=2).astype(BF16)
    uq = mla_w_uq.reshape(L, MLA_Q_RANK, MLA_HEADS, MLA_NOPE + MLA_ROPE)
    wuq = jnp.pad(uq, ((0, 0), (0, 0), (0, 0), (0, HEAD_PAD - MLA_NOPE - MLA_ROPE))).reshape(L, MLA_Q_RANK, -1)
    wuk = jnp.pad(mla_w_uk, ((0, 0), (0, 0), (0, 0), (0, HEAD_PAD - MLA_NOPE))).reshape(L, MLA_KV_RANK, -1)
    wuv = mla_w_uv.reshape(L, MLA_KV_RANK, MLA_HEADS * MLA_V)
    grp = jnp.arange(512) // GQA_DIM
    bd = (grp[:, None] == grp[None, :]).astype(BF16)
    wr = jnp.pad(w_router, ((0, 0), (0, 0), (0, LANES - N_EXPERTS)))
    wr_hi = wr.astype(BF16)
    wr_lo = (wr - wr_hi.astype(F32)).astype(BF16)
    vec = lambda a: a.reshape(L, 1, -1)
    return {
        "norm1": vec(norm1), "norm2": vec(norm2), "w1": w1,
        "mla_q_norm": vec(mla_q_norm), "mla_kv_norm": vec(mla_kv_norm),
        "gqa_q_norm": vec(jnp.tile(gqa_q_norm, (1, GQA_HEADS))),
        "gqa_k_norm": vec(jnp.tile(gqa_k_norm, (1, GQA_KV_HEADS))),
        "wuq": wuq.astype(BF16), "wuk": wuk.astype(BF16), "wuv": wuv.astype(BF16), "bd": bd,
        "wgate": w_in[:, :, 2976:].astype(BF16), "wbr_mla": w_br_mla.astype(BF16),
        "wbr_gqa": w_br_gqa.astype(BF16), "wbr_diff": w_br_diff.astype(BF16),
        "w_out": w_out.astype(BF16), "wr_hi": wr_hi, "wr_lo": wr_lo,
    }


def _trunk(x, mod_all, msel, W, l, ew, lam, lam_init, tabs, ctx, B, T, final_norm, final, own):
    n = B * T
    outs = _in_proj(x, mod_all, msel, W, l, tabs, T, own)
    qm, km, vm, gq, gk, gv, dq, dk, dv = [a.reshape(B, T, -1) for a in outs[:9]]
    if ctx is None:
        pm, pg, pd = [(km, vm)], [(gk, gv)], [(dk, dv)]
    else:
        ckm, cvm, cgk, cgv, cdk, cdv = ctx
        pm, pg, pd = [(ckm, cvm), (km, vm)], [(cgk, cgv), (gk, gv)], [(cdk, cdv), (dk, dv)]
    om = _attention("mla", qm, pm, []).reshape(n, -1)
    og = _attention("gqa", gq, pg, []).reshape(n, -1)
    od = _attention("diff", dq, pd, list(lam), l, lam_init).reshape(n, -1)
    x1, h2, aff = _out_proj(x, mod_all, msel, om, og, od, W, l, T)
    cap = CAPACITY_FACTOR * n // N_EXPERTS
    idx, gate = _route(aff, cap)
    tc = min(MOE_ROW_TILE, cap)
    xe = _gather(idx, h2, tc)
    ye = _ffn(xe, ew[0], ew[1], ew[2], ew[3], min(FFN_TILE, cap))
    xn = _combine(idx, gate, ye, x1, mod_all, msel, l, final_norm, tc, T, final)
    return xn, outs[9:]


def kernel(x_prompt, x_sample, cache_mla_ckv, cache_mla_kpe, cache_gqa_k, cache_gqa_v, cache_diff_k, cache_diff_v, c, c_ctx, w_ada, b_ada, norm1, norm2, w_in, mla_q_norm, mla_kv_norm, mla_w_uq, mla_w_uk, mla_w_uv, gqa_q_norm, gqa_k_norm, diff_lam_q1, diff_lam_k1, diff_lam_q2, diff_lam_k2, diff_subln, w_br_mla, w_br_gqa, w_br_diff, w_out, w_router, w_gate, w_up, w_down, final_norm):
    Bp, Tp, D = x_prompt.shape
    Bs, Ts, _ = x_sample.shape
    L = w_in.shape[0]
    P = cache_mla_ckv.shape[2]
    assert D == D_MODEL and Ts % GRID_W == 0

    nrow = -(-(1 + Bs) // 8) * 8
    cond = jnp.zeros((nrow, D), F32).at[0].set(c_ctx).at[1:1 + Bs].set(c)
    mod_all = _ada(cond, w_ada, b_ada).reshape(L, nrow, 6, D)

    tabs = _rope_tables(Ts)
    fn = final_norm[None]
    xp = x_prompt.reshape(Bp * Tp, D)
    xs = x_sample.reshape(Bs * Ts, D)
    owns = []
    W = _prep_weights(w_in, norm1, norm2, mla_q_norm, mla_kv_norm, mla_w_uq, mla_w_uk, mla_w_uv,
                      gqa_q_norm, gqa_k_norm, w_br_mla, w_br_gqa, w_br_diff, w_out, w_router)
    lam = tuple(a.reshape(L, 1, -1) for a in (diff_lam_q1, diff_lam_k1, diff_lam_q2, diff_lam_k2, diff_subln))
    kpe_pad_all = jnp.pad(cache_mla_kpe, ((0, 0), (0, 0), (0, 0), (0, LANES - MLA_ROPE)))
    for l in range(L):
        ew = (w_gate, w_up, w_down, l)
        lam_init = 0.8 - 0.6 * math.exp(-0.3 * l)
        final = l == L - 1
        xp, own = _trunk(xp, mod_all, (0, False), W, l, ew, lam, lam_init, None, None, Bp, Tp, fn, final, True)
        owns.append(own)
        kpe_pad = kpe_pad_all[:, l].reshape(Bs * P, LANES)
        ckm, cvm = _cache_proj(cache_mla_ckv[:, l].reshape(Bs * P, MLA_KV_RANK), kpe_pad, W, l)
        ctx = (ckm.reshape(Bs, P, -1), cvm.reshape(Bs, P, -1),
               cache_gqa_k[:, l].reshape(Bs, P, -1), cache_gqa_v[:, l].reshape(Bs, P, -1),
               cache_diff_k[:, l].reshape(Bs, P, -1), cache_diff_v[:, l].reshape(Bs, P, -1))
        xs, _ = _trunk(xs, mod_all, (1, True), W, l, ew, lam, lam_init, tabs, ctx, Bs, Ts, fn, final, False)

    def stack(k, shape):
        return jnp.stack([o[k].reshape((Bp, Tp) + shape) for o in owns], axis=1)

    return (xp.reshape(Bp, Tp, D), xs.reshape(Bs, Ts, D),
            stack(0, (MLA_KV_RANK,)), stack(1, (MLA_ROPE,)),
            stack(2, (GQA_KV_HEADS, GQA_DIM)), stack(3, (GQA_KV_HEADS, GQA_DIM)),
            stack(4, (DIFF_HEADS, 2 * DIFF_DIM)), stack(5, (DIFF_HEADS, DIFF_V)))
```

```python
import functools
import math

import jax
import jax.numpy as jnp
from jax import lax
from jax.experimental import pallas as pl
from jax.experimental.pallas import tpu as pltpu

F32 = jnp.float32
BF16 = jnp.bfloat16
I32 = jnp.int32

D_MODEL = 1024
GRID_W = 64
ROPE_BASE = 10000.0
EPS = 1e-6
MLA_HEADS, MLA_NOPE, MLA_ROPE, MLA_V = 8, 64, 32, 64
MLA_Q_RANK, MLA_KV_RANK = 384, 256
GQA_HEADS, GQA_KV_HEADS, GQA_DIM = 8, 2, 64
DIFF_HEADS, DIFF_DIM = 4, 64
DIFF_V = 2 * DIFF_DIM
N_EXPERTS, EXPERT_FF, CAPACITY_FACTOR = 16, 1024, 2

LANES = 128
SUBLANES = 8
HEAD_PAD = 128
ROW_TILE = 512
IN_SUB_TILE = 256
OUT_ROW_TILE = 512
OUT_SUB_TILE = 256
Q_TILE = 1024
ATTN_ROWS_PER_STEP = 1024
FFN_TILE = 1024
MOE_ROW_TILE = 512
VMEM_LIMIT = 56 * 1024 * 1024

_O_CQ, _O_CKV, _O_GQ, _O_GK, _O_GV, _O_DQ, _O_DK, _O_DV, _O_KPE, _W1_COLS = (
    0, 384, 640, 1152, 1280, 1408, 1920, 2432, 2944, 3072)
LOG2E = 1.4426950408889634


def _cparams(sem):
    return pltpu.CompilerParams(dimension_semantics=sem, vmem_limit_bytes=VMEM_LIMIT)


def _dot(a, b):
    return jnp.dot(a, b, preferred_element_type=F32)


def _dot_nt(a, b):
    return lax.dot_general(a, b, (((1,), (1,)), ((), ())), preferred_element_type=F32)


def _rms(x, g):
    return x * lax.rsqrt(jnp.mean(x * x, axis=-1, keepdims=True) + EPS) * g


def _lane_iota(shape):
    return lax.broadcasted_iota(I32, shape, len(shape) - 1)


def _rope(x, cos, sin, width):
    outs = []
    even = ((_lane_iota((1, LANES)) // width) % 2) == 0
    for j in range(x.shape[1] // LANES):
        xb = x[:, j * LANES:(j + 1) * LANES]
        partner = jnp.where(even, pltpu.roll(xb, LANES - width, 1), pltpu.roll(xb, width, 1))
        outs.append(xb * cos + partner * sin)
    return outs[0] if len(outs) == 1 else jnp.concatenate(outs, axis=1)


def _group_mean_sq(x, bd):
    sq = x * x
    hi = sq.astype(BF16)
    lo = (sq - hi.astype(F32)).astype(BF16)
    return (_dot(hi, bd) + _dot(lo, bd)) * (1.0 / GQA_DIM)


def _ada_body(c_ref, w_ref, b_ref, o_ref):
    c = c_ref[...]
    s = c / (1.0 + jnp.exp(-c))
    o_ref[0] = _dot(s.astype(BF16), w_ref[0].astype(BF16)) + b_ref[0]


def _ada(cond, w_ada, b_ada):
    L, D, N = w_ada.shape
    R = cond.shape[0]
    tn = 512
    return pl.pallas_call(
        _ada_body,
        out_shape=jax.ShapeDtypeStruct((L, R, N), F32),
        grid=(L, N // tn),
        in_specs=[pl.BlockSpec((R, D), lambda l, j: (0, 0)),
                  pl.BlockSpec((1, D, tn), lambda l, j: (l, 0, j)),
                  pl.BlockSpec((1, 1, tn), lambda l, j: (l, 0, j))],
        out_specs=pl.BlockSpec((1, R, tn), lambda l, j: (l, 0, j)),
        compiler_params=_cparams(("parallel", "parallel")),
        name="ada",
    )(cond, w_ada, b_ada.reshape(L, 1, N))


def _in_body(rope, own, *refs):
    (x_ref, mod_ref, n1_ref, w_ref, qn_ref, kvn_ref, gqn_ref, gkn_ref, wuq_ref, wuk_ref, wuv_ref,
     bd_ref) = refs[:12]
    k = 12
    if rope:
        c64_ref, s64_ref, cm_ref, sm_ref = refs[k:k + 4]
        k += 4
    (qm_ref, km_ref, vm_ref, gq_ref, gk_ref, gv_ref, dq_ref, dk_ref, dv_ref) = refs[k:k + 9]
    k += 9
    if own:
        ockv_ref, okpe_ref, ogk_ref, ogv_ref, odk_ref, odv_ref = refs[k:k + 6]

    shift = mod_ref[0, 0:1, :]
    scale = mod_ref[0, 1:2, :]
    bd = bd_ref[...]
    sub = min(IN_SUB_TILE, x_ref.shape[0])
    for st in range(x_ref.shape[0] // sub):
        rows = pl.ds(st * sub, sub)
        x = x_ref[rows, :]
        h = _rms(x, n1_ref[...]) * (1.0 + scale) + shift
        z = _dot(h.astype(BF16), w_ref[...])

        def rot64(u):
            return _rope(u, c64_ref[rows, :], s64_ref[rows, :], 16) if rope else u

        def rotm(u):
            return _rope(u, cm_ref[rows, :], sm_ref[rows, :], 8) if rope else u

        cq = _rms(z[:, _O_CQ:_O_CKV], qn_ref[...])
        q = _dot(cq.astype(BF16), wuq_ref[...])
        qm_ref[rows, :] = (rotm(q) * (LOG2E * (MLA_NOPE + MLA_ROPE) ** -0.5)).astype(BF16)
        ckv = _rms(z[:, _O_CKV:_O_GQ], kvn_ref[...])
        kpe = z[:, _O_KPE:_W1_COLS]
        ckv_b = ckv.astype(BF16)
        kpe_at = pltpu.roll(kpe, MLA_NOPE, 1)
        knope = _dot(ckv_b, wuk_ref[...])
        kfull = knope + jnp.concatenate([kpe_at] * MLA_HEADS, axis=1)
        km_ref[rows, :] = rotm(kfull).astype(BF16)
        vm_ref[rows, :] = _dot(ckv_b, wuv_ref[...]).astype(BF16)
        gq = z[:, _O_GQ:_O_GK]
        gq = gq * lax.rsqrt(_group_mean_sq(gq, bd) + EPS) * gqn_ref[...]
        gq_ref[rows, :] = (rot64(gq) * (LOG2E * GQA_DIM ** -0.5)).astype(BF16)
        gk = z[:, _O_GK:_O_GV]
        gk = gk * lax.rsqrt(_group_mean_sq(gk, bd[:LANES, :LANES]) + EPS) * gkn_ref[...]
        gk_ref[rows, :] = rot64(gk).astype(BF16)
        gv = z[:, _O_GV:_O_DQ]
        gv_ref[rows, :] = gv.astype(BF16)
        dq_ref[rows, :] = (rot64(z[:, _O_DQ:_O_DK]) * (LOG2E * DIFF_DIM ** -0.5)).astype(BF16)
        dk = z[:, _O_DK:_O_DV]
        dk_ref[rows, :] = rot64(dk).astype(BF16)
        dv = z[:, _O_DV:_O_KPE]
        dv_ref[rows, :] = dv.astype(BF16)
        if own:
            ockv_ref[rows, :] = ckv
            okpe_ref[rows, :] = kpe[:, :MLA_ROPE]
            ogk_ref[rows, :] = gk
            ogv_ref[rows, :] = gv
            odk_ref[rows, :] = dk
            odv_ref[rows, :] = dv


def _layer_spec(a, l):
    zeros = (0,) * (a.ndim - 1)
    return pl.BlockSpec((None,) + a.shape[1:], lambda *_: (l,) + zeros)


def _mod_spec(mod_all, msel, l, tpb, tile_of=lambda i: i):
    row0, per_batch = msel
    D = mod_all.shape[-1]
    if per_batch:
        return pl.BlockSpec((None, 1, 6, D), lambda i, *_: (l, row0 + tile_of(i) // tpb, 0, 0))
    return pl.BlockSpec((None, 1, 6, D), lambda i, *_: (l, row0, 0, 0))


def _row_tile(n, T, msel, tile=ROW_TILE):
    return min(tile, T) if msel[1] else min(tile, n)


def _in_proj(x, mod_all, msel, W, l, tabs, T, own):
    n, D = x.shape
    tm = _row_tile(n, T, msel)
    rope = tabs is not None
    tpb = max(T // tm, 1)
    row = lambda i: (i, 0)
    stacked = [W[k] for k in ("norm1", "w1", "mla_q_norm", "mla_kv_norm", "gqa_q_norm", "gqa_k_norm",
                              "wuq", "wuk", "wuv")]
    ins = [x, mod_all] + stacked + [W["bd"]]
    in_specs = ([pl.BlockSpec((tm, D), row), _mod_spec(mod_all, msel, l, tpb)]
                + [_layer_spec(a, l) for a in stacked] + [pl.BlockSpec(W["bd"].shape, lambda i: (0, 0))])
    if rope:
        ins += list(tabs)
        in_specs += [pl.BlockSpec((tm, LANES), lambda i: (i % tpb, 0))] * 4
    widths = [(1024, BF16), (1024, BF16), (512, BF16), (512, BF16), (128, BF16), (128, BF16),
              (512, BF16), (512, BF16), (512, BF16)]
    if own:
        widths += [(256, F32), (32, F32), (128, F32), (128, F32), (512, F32), (512, F32)]
    out_shape = [jax.ShapeDtypeStruct((n, w), dt) for w, dt in widths]
    out_specs = [pl.BlockSpec((tm, w), row) for w, _ in widths]
    return pl.pallas_call(
        functools.partial(_in_body, rope, own),
        out_shape=out_shape, grid=(n // tm,), in_specs=in_specs, out_specs=out_specs,
        compiler_params=_cparams(("parallel",)), name="in_proj",
    )(*ins)


def _cache_body(ckv_ref, kpe_ref, wuk_ref, wuv_ref, km_ref, vm_ref):
    ckv_b = ckv_ref[...].astype(BF16)
    kpe_at = pltpu.roll(kpe_ref[...], MLA_NOPE, 1)
    km_ref[...] = (_dot(ckv_b, wuk_ref[...]) + jnp.concatenate([kpe_at] * MLA_HEADS, axis=1)).astype(BF16)
    vm_ref[...] = _dot(ckv_b, wuv_ref[...]).astype(BF16)


def _cache_proj(ckv, kpe_pad, W, l):
    n = ckv.shape[0]
    tm = min(ROW_TILE, n)
    row = lambda i: (i, 0)
    return pl.pallas_call(
        _cache_body,
        out_shape=[jax.ShapeDtypeStruct((n, 1024), BF16), jax.ShapeDtypeStruct((n, 512), BF16)],
        grid=(n // tm,),
        in_specs=[pl.BlockSpec((tm, MLA_KV_RANK), row), pl.BlockSpec((tm, LANES), row),
                  _layer_spec(W["wuk"], l), _layer_spec(W["wuv"], l)],
        out_specs=[pl.BlockSpec((tm, 1024), row), pl.BlockSpec((tm, 512), row)],
        compiler_params=_cparams(("parallel",)), name="cache_proj",
    )(ckv, kpe_pad, W["wuk"], W["wuv"])


def _swap_halves(a):
    return pltpu.roll(a.astype(F32), LANES // 2, 1).astype(a.dtype)


def _softmax_av(q, ks, vs, v_has_ones):
    ss = [_dot_nt(q, k) for k in ks]
    mx = ss[0].max(axis=-1, keepdims=True)
    for s in ss[1:]:
        mx = jnp.maximum(mx, s.max(axis=-1, keepdims=True))
    acc, den = None, None
    for s, v in zip(ss, vs):
        p = jnp.exp2(s - mx)
        if not v_has_ones:
            d = p.sum(axis=-1, keepdims=True)
            den = d if den is None else den + d
        a = _dot(p.astype(BF16), v)
        acc = a if acc is None else acc + a
    return acc, den


def _attn_body(mode, nparts, lam_init, *refs):
    q_ref = refs[0]
    kv = [(refs[1 + 2 * i], refs[2 + 2 * i]) for i in range(nparts)]
    k = 1 + 2 * nparts
    if mode == "diff":
        lq1, lk1, lq2, lk2, sub_ref = refs[k:k + 5]
        k += 5
    o_ref = refs[k]

    lo = _lane_iota((1, LANES)) < (LANES // 2)
    hi = jnp.logical_not(lo)
    ones = jnp.ones((1, LANES), BF16)
    cols = lambda j: slice(j * LANES, (j + 1) * LANES)

    def normalise(full):
        return full / pltpu.roll(full, LANES // 2, 1)

    if mode == "diff":
        lam = (jnp.exp(jnp.sum(lq1[...] * lk1[...], axis=-1, keepdims=True))
               - jnp.exp(jnp.sum(lq2[...] * lk2[...], axis=-1, keepdims=True)) + lam_init)
    for bi in range(q_ref.shape[0]):
        if mode == "gqa":
            ks = [kr[bi].astype(BF16) for kr, _ in kv]
            vs = [vr[bi].astype(BF16) for _, vr in kv]
            ks_sw = [_swap_halves(a) for a in ks]
            vs_sw = [_swap_halves(a) for a in vs]
            for pair in range(GQA_HEADS // 2):
                qp = q_ref[bi, :, cols(pair)]
                res = []
                for L in range(2):
                    g = (2 * pair + L) // (GQA_HEADS // GQA_KV_HEADS)
                    mine = lo if L == 0 else hi
                    kk = ks if g == L else ks_sw
                    vv = [jnp.where(mine, a, ones) for a in (vs if g == L else vs_sw)]
                    full, _ = _softmax_av(jnp.where(mine, qp, jnp.zeros_like(qp)), kk, vv, True)
                    res.append(normalise(full))
                o_ref[bi, :, cols(pair)] = jnp.where(lo, res[0], res[1]).astype(o_ref.dtype)
        elif mode == "mla":
            for pair in range(MLA_HEADS // 2):
                vs = [vr[bi, :, cols(pair)] for _, vr in kv]
                res = []
                for L in range(2):
                    mine = lo if L == 0 else hi
                    kk = [kr[bi, :, cols(2 * pair + L)] for kr, _ in kv]
                    vv = [jnp.where(mine, a, ones) for a in vs]
                    full, _ = _softmax_av(q_ref[bi, :, cols(2 * pair + L)], kk, vv, True)
                    res.append(normalise(full))
                o_ref[bi, :, cols(pair)] = jnp.where(lo, res[0], res[1]).astype(o_ref.dtype)
        else:
            for h in range(DIFF_HEADS):
                qh = q_ref[bi, :, cols(h)]
                kk = [kr[bi, :, cols(h)].astype(BF16) for kr, _ in kv]
                vv = [vr[bi, :, cols(h)].astype(BF16) for _, vr in kv]
                zero = jnp.zeros_like(qh)
                a1, d1 = _softmax_av(jnp.where(lo, qh, zero), kk, vv, False)
                a2, d2 = _softmax_av(jnp.where(lo, zero, qh), kk, vv, False)
                o = _rms(a1 / d1 - lam * (a2 / d2), sub_ref[...]) * (1.0 - lam_init)
                o_ref[bi, :, cols(h)] = o.astype(o_ref.dtype)


def _attention(mode, q, parts, extra, l=0, lam_init=0.0):
    B, T, C = q.shape
    tq = min(Q_TILE, T)
    per_step = {"diff": ATTN_ROWS_PER_STEP // tq, "gqa": 2, "mla": 1}[mode]
    bb = max(1, min(B, per_step)) if (T == tq and T < Q_TILE) else 1
    while B % bb:
        bb -= 1
    ins = [q]
    in_specs = [pl.BlockSpec((bb, tq, C), lambda b, i: (b, i, 0))]
    for k, v in parts:
        ins += [k, v]
        in_specs += [pl.BlockSpec((bb,) + k.shape[1:], lambda b, i: (b, 0, 0)),
                     pl.BlockSpec((bb,) + v.shape[1:], lambda b, i: (b, 0, 0))]
    for e in extra:
        ins.append(e)
        in_specs.append(_layer_spec(e, l))
    return pl.pallas_call(
        functools.partial(_attn_body, mode, len(parts), lam_init),
        out_shape=jax.ShapeDtypeStruct((B, T, 512), BF16),
        grid=(B // bb, T // tq),
        in_specs=in_specs,
        out_specs=pl.BlockSpec((bb, tq, 512), lambda b, i: (b, i, 0)),
        compiler_params=_cparams(("parallel", "parallel")), name="attn_" + mode,
    )(*ins)


def _out_body(x_ref, mod_ref, n1_ref, n2_ref, om_ref, og_ref, od_ref, wg_ref, bm_ref, bg_ref, bdf_ref,
              wo_ref, rh_ref, rl_ref, x1_ref, h2_ref, aff_ref):
    shift1, scale1, gate1 = mod_ref[0, 0:1, :], mod_ref[0, 1:2, :], mod_ref[0, 2:3, :]
    shift2, scale2 = mod_ref[0, 3:4, :], mod_ref[0, 4:5, :]
    D = x_ref.shape[1]
    sig = lambda u: 1.0 / (1.0 + jnp.exp(-u))
    valid = _lane_iota((1, LANES)) < N_EXPERTS
    sub = min(OUT_SUB_TILE, x_ref.shape[0])
    for s in range(x_ref.shape[0] // sub):
        rows = pl.ds(s * sub, sub)
        x = x_ref[rows, :]
        h = (_rms(x, n1_ref[...]) * (1.0 + scale1) + shift1).astype(BF16)
        merged = None
        for b, (o_ref, w_ref) in enumerate(((om_ref, bm_ref), (og_ref, bg_ref), (od_ref, bdf_ref))):
            t = sig(_dot(h, wg_ref[:, b * D:(b + 1) * D])) * _dot(o_ref[rows, :], w_ref[...])
            merged = t if merged is None else merged + t
        y = _dot(merged.astype(BF16), wo_ref[...])
        x1 = x + gate1 * y
        x1_ref[rows, :] = x1
        h2 = _rms(x1, n2_ref[...]) * (1.0 + scale2) + shift2
        h2_ref[rows, :] = h2
        hi = h2.astype(BF16)
        lo = (h2 - hi.astype(F32)).astype(BF16)
        logits = _dot(hi, rh_ref[...]) + _dot(lo, rh_ref[...]) + _dot(hi, rl_ref[...])
        logits = jnp.where(valid, logits, -1e30)
        e = jnp.exp(logits - logits.max(axis=-1, keepdims=True))
        aff = e / e.sum(axis=-1, keepdims=True)
        aff_ref[:, s * sub:(s + 1) * sub] = aff.T[:N_EXPERTS, :]


def _out_proj(x, mod_all, msel, om, og, od, W, l, T):
    n, D = x.shape
    tm = _row_tile(n, T, msel, OUT_ROW_TILE)
    tpb = max(T // tm, 1)
    row = lambda i: (i, 0)
    ws = [W[k] for k in ("wgate", "wbr_mla", "wbr_gqa", "wbr_diff", "w_out", "wr_hi", "wr_lo")]
    return pl.pallas_call(
        _out_body,
        out_shape=[jax.ShapeDtypeStruct((n, D), F32), jax.ShapeDtypeStruct((n, D), F32),
                   jax.ShapeDtypeStruct((N_EXPERTS, n), F32)],
        grid=(n // tm,),
        in_specs=[pl.BlockSpec((tm, D), row), _mod_spec(mod_all, msel, l, tpb), _layer_spec(W["norm1"], l),
                  _layer_spec(W["norm2"], l), pl.BlockSpec((tm, 512), row), pl.BlockSpec((tm, 512), row),
                  pl.BlockSpec((tm, 512), row)] + [_layer_spec(w, l) for w in ws],
        out_specs=[pl.BlockSpec((tm, D), row), pl.BlockSpec((tm, D), row),
                   pl.BlockSpec((N_EXPERTS, tm), lambda i: (0, i))],
        compiler_params=_cparams(("parallel",)), name="out_proj",
    )(x, mod_all, W["norm1"], W["norm2"], om, og, od, *ws)


def _route_body(cap, aff_ref, idx_ref, gate_ref):
    v = aff_ref[...]
    E, n = v.shape
    nbits = max(1, (n - 1).bit_length())
    lane = _lane_iota((E, n))

    def count(mask):
        return jnp.sum(mask.astype(I32), axis=1, keepdims=True)

    def step(i, t):
        cand = t | lax.shift_left(jnp.int32(1), 30 - i)
        return jnp.where(count(v >= lax.bitcast_convert_type(cand, F32)) >= cap, cand, t)

    thr = lax.bitcast_convert_type(lax.fori_loop(0, 31, step, jnp.zeros((E, 1), I32)), F32)

    def prefix(u):
        for b in range(nbits):
            s = 1 << b
            u = u + jnp.where(lane >= s, pltpu.roll(u, s, 1), 0)
        return u

    gt = v > thr
    eq = v == thr
    need = cap - count(gt)
    eq_i = eq.astype(I32)
    eq_rank = prefix(eq_i) - eq_i
    sel = jnp.where(gt | (eq & (eq_rank < need)), 1, 0).astype(I32)
    dist = lane + 1 - prefix(sel)
    tok = lane
    for b in range(nbits):
        s = 1 << b
        mv = sel * ((dist >> b) & 1)
        stay = sel - mv
        take = pltpu.roll(mv, n - s, 1) == 1
        tok = jnp.where(take, pltpu.roll(tok, n - s, 1), tok)
        v = jnp.where(take, pltpu.roll(v, n - s, 1), v)
        dist = jnp.where(take, pltpu.roll(dist, n - s, 1), dist)
        sel = jnp.where(take, 1, stay)
    idx_ref[...] = tok[:, :cap]
    gate_ref[...] = v[:, :cap]


def _route(aff_t, cap):
    E, n = aff_t.shape
    return pl.pallas_call(
        functools.partial(_route_body, cap),
        out_shape=[jax.ShapeDtypeStruct((E, cap), I32), jax.ShapeDtypeStruct((E, cap), F32)],
        compiler_params=pltpu.CompilerParams(vmem_limit_bytes=VMEM_LIMIT), name="route",
    )(aff_t)


def _gather_body(tc, tiles_per_e, idx_ref, xs_ref, o_ref):
    i = pl.program_id(0)
    base = i * tc

    group = 16
    D = xs_ref.shape[1]
    sub = lax.broadcasted_iota(I32, (SUBLANES, D), 0)

    def body(g, _):
        halves = []
        for hf in range(group // SUBLANES):
            acc = jnp.zeros((SUBLANES, D), F32)
            for j in range(SUBLANES):
                r = idx_ref[base + g * group + hf * SUBLANES + j]
                r8 = pl.multiple_of((r >> 3) << 3, SUBLANES)
                blk8 = pltpu.roll(xs_ref[pl.ds(r8, SUBLANES), :], (j - r) & (SUBLANES - 1), 0)
                acc = jnp.where(sub == j, blk8, acc)
            halves.append(acc)
        rows = pl.ds(pl.multiple_of(g * group, group), group)
        o_ref[rows, :] = jnp.concatenate(halves, axis=0).astype(BF16)
        return 0

    lax.fori_loop(0, tc // group, body, 0)


def _gather(idx, xs, tc):
    E, cap = idx.shape
    n, D = xs.shape
    tiles = cap // tc
    return pl.pallas_call(
        functools.partial(_gather_body, tc, tiles),
        out_shape=jax.ShapeDtypeStruct((E * cap, D), BF16),
        grid_spec=pltpu.PrefetchScalarGridSpec(
            num_scalar_prefetch=1, grid=(E * tiles,),
            in_specs=[pl.BlockSpec((n, D), lambda i, idx: (0, 0), pipeline_mode=pl.Buffered(1))],
            out_specs=pl.BlockSpec((tc, D), lambda i, idx: (i, 0))),
        compiler_params=_cparams(("arbitrary",)), name="moe_gather",
    )(idx.reshape(-1), xs)


def _ffn_body(xe_ref, wg_ref, wu_ref, wd_ref, o_ref):
    xe = xe_ref[...]
    hg = _dot(xe, wg_ref[0, 0].astype(BF16))
    hu = _dot(xe, wu_ref[0, 0].astype(BF16))
    a = (hg / (1.0 + jnp.exp(-hg))) * hu
    o_ref[...] = _dot(a.astype(BF16), wd_ref[0, 0].astype(BF16))


def _ffn(xe, wg, wu, wd, l, tc):
    _, E, D, FF = wg.shape
    rows = xe.shape[0]
    tiles = rows // E // tc
    wmap = lambda e, t: (l, e, 0, 0)
    row = lambda e, t: (e * tiles + t, 0)
    return pl.pallas_call(
        _ffn_body,
        out_shape=jax.ShapeDtypeStruct((rows, D), F32),
        grid=(E, tiles),
        in_specs=[pl.BlockSpec((tc, D), row), pl.BlockSpec((1, 1, D, FF), wmap),
                  pl.BlockSpec((1, 1, D, FF), wmap), pl.BlockSpec((1, 1, FF, D), wmap)],
        out_specs=pl.BlockSpec((tc, D), row),
        compiler_params=_cparams(("parallel", "parallel")), name="moe_ffn",
    )(xe, wg, wu, wd)


def _combine_body(tc, tiles_per_e, n_scatter, final, idx_ref, gate_ref, ye_ref, x1_ref, mod_ref, fn_ref,
                  o_ref, acc_ref):
    i = pl.program_id(0)

    @pl.when(i == 0)
    def _():
        acc_ref[...] = jnp.zeros_like(acc_ref)

    @pl.when(i < n_scatter)
    def _():
        base = i * tc

        sub = lax.broadcasted_iota(I32, (SUBLANES, LANES), 0)

        nlane = ye_ref.shape[1] // LANES
        ways = 4

        def body(gq, _):
            s0 = pl.multiple_of(gq * (ways * SUBLANES), ways * SUBLANES)
            ye8 = [ye_ref[pl.ds(s0 + w * SUBLANES, SUBLANES), :] for w in range(ways)]
            for j in range(SUBLANES):
                upd = []
                for w in range(ways):
                    r = idx_ref[base + s0 + w * SUBLANES + j]
                    g = gate_ref[base + s0 + w * SUBLANES + j]
                    r8 = pl.multiple_of((r >> 3) << 3, SUBLANES)
                    rolled = pltpu.roll(ye8[w], (r - j) & (SUBLANES - 1), 0)
                    gmask = jnp.where(sub == (r & (SUBLANES - 1)), g, 0.0)
                    upd.append((r8, acc_ref[pl.ds(r8, SUBLANES), :]
                                + rolled * jnp.concatenate([gmask] * nlane, axis=1)))
                for r8, val in upd:
                    acc_ref[pl.ds(r8, SUBLANES), :] = val
            return 0

        lax.fori_loop(0, tc // (ways * SUBLANES), body, 0)

    @pl.when(i >= n_scatter)
    def _():
        tm = x1_ref.shape[0]
        j = i - n_scatter
        rows = pl.ds(pl.multiple_of(j * tm, tm), tm)
        xn = x1_ref[...] + mod_ref[0, 5:6, :] * acc_ref[rows, :]
        o_ref[...] = _rms(xn, fn_ref[...]) if final else xn


def _combine(idx, gate, ye, x1, mod_all, msel, l, final_norm, tc, T, final):
    E, cap = idx.shape
    n, D = x1.shape
    tm = _row_tile(n, T, msel)
    tiles = cap // tc
    n_sc = E * tiles
    tpb = max(T // tm, 1)
    out_tile = lambda i: jnp.maximum(i - n_sc, 0)
    out_row = lambda i, a, b: (out_tile(i), 0)
    return pl.pallas_call(
        functools.partial(_combine_body, tc, tiles, n_sc, final),
        out_shape=jax.ShapeDtypeStruct((n, D), F32),
        grid_spec=pltpu.PrefetchScalarGridSpec(
            num_scalar_prefetch=2, grid=(n_sc + n // tm,),
            in_specs=[pl.BlockSpec((tc, D), lambda i, a, b: (jnp.minimum(i, n_sc - 1), 0)),
                      pl.BlockSpec((tm, D), out_row), _mod_spec(mod_all, msel, l, tpb, out_tile),
                      pl.BlockSpec((1, D), lambda i, a, b: (0, 0))],
            out_specs=pl.BlockSpec((tm, D), out_row),
            scratch_shapes=[pltpu.VMEM((n, D), F32)]),
        compiler_params=_cparams(("arbitrary",)), name="moe_combine",
    )(idx.reshape(-1), gate.reshape(-1), ye, x1, mod_all, final_norm)


def _rope_tables(T):
    t = jnp.arange(T)
    row = (t // GRID_W).astype(F32)
    col = (t % GRID_W).astype(F32)

    def table(head_w, lane_off, rep):
        half = head_w // 4
        freqs = ROPE_BASE ** (-jnp.arange(half, dtype=F32) / half)
        blk = []
        for pos in (row, col):
            ang = pos[:, None] * freqs[None, :]
            blk.append((jnp.cos(ang), jnp.sin(ang)))
        cos = jnp.concatenate([blk[0][0], blk[0][0], blk[1][0], blk[1][0]], axis=1)
        sin = jnp.concatenate([-blk[0][1], blk[0][1], -blk[1][1], blk[1][1]], axis=1)
        cos = jnp.concatenate([cos] * rep, axis=1)
        sin = jnp.concatenate([sin] * rep, axis=1)
        pad_r = LANES - lane_off - cos.shape[1]
        cos = jnp.pad(cos, ((0, 0), (lane_off, pad_r)), constant_values=1.0)
        sin = jnp.pad(sin, ((0, 0), (lane_off, pad_r)))
        return cos, sin

    c64, s64 = table(GQA_DIM, 0, 2)
    cm, sm = table(MLA_ROPE, MLA_NOPE, 1)
    return c64, s64, cm, sm


def _prep_weights(w_in, norm1, norm2, mla_q_norm, mla_kv_norm, mla_w_uq, mla_w_uk, mla_w_uv,
                  gqa_q_norm, gqa_k_norm, w_br_mla, w_br_gqa, w_br_diff, w_out, w_router):
    L, D = w_in.shape[0], D_MODEL
    seg = lambda a, b: w_in[:, :, a:b]
    w1 = jnp.concatenate([seg(0, 640), seg(672, 2976), seg(640, 672),
                          jnp.zeros((L, D, _W1_COLS - _O_KPE - MLA_ROPE), F32)], axis=2).astype(BF16)
    uq = mla_w_uq.reshape(L, MLA_Q_RANK, MLA_HEADS, MLA_NOPE + MLA_ROPE)
    wuq = jnp.pad(uq, ((0, 0), (0, 0), (0, 0), (0, HEAD_PAD - MLA_NOPE - MLA_ROPE))).reshape(L, MLA_Q_RANK, -1)
    wuk = jnp.pad(mla_w_uk, ((0, 0), (0, 0), (0, 0), (0, HEAD_PAD - MLA_NOPE))).reshape(L, MLA_KV_RANK, -1)
    wuv = mla_w_uv.reshape(L, MLA_KV_RANK, MLA_HEADS * MLA_V)
    grp = jnp.arange(512) // GQA_DIM
    bd = (grp[:, None] == grp[None, :]).astype(BF16)
    wr = jnp.pad(w_router, ((0, 0), (0, 0), (0, LANES - N_EXPERTS)))
    wr_hi = wr.astype(BF16)
    wr_lo = (wr - wr_hi.astype(F32)).astype(BF16)
    vec = lambda a: a.reshape(L, 1, -1)
    return {
        "norm1": vec(norm1), "norm2": vec(norm2), "w1": w1,
        "mla_q_norm": vec(mla_q_norm), "mla_kv_norm": vec(mla_kv_norm),
        "gqa_q_norm": vec(jnp.tile(gqa_q_norm, (1, GQA_HEADS))),
        "gqa_k_norm": vec(jnp.tile(gqa_k_norm, (1, GQA_KV_HEADS))),
        "wuq": wuq.astype(BF16), "wuk": wuk.astype(BF16), "wuv": wuv.astype(BF16), "bd": bd,
        "wgate": w_in[:, :, 2976:].astype(BF16), "wbr_mla": w_br_mla.astype(BF16),
        "wbr_gqa": w_br_gqa.astype(BF16), "wbr_diff": w_br_diff.astype(BF16),
        "w_out": w_out.astype(BF16), "wr_hi": wr_hi, "wr_lo": wr_lo,
    }


def _trunk(x, mod_all, msel, W, l, ew, lam, lam_init, tabs, ctx, B, T, final_norm, final, own):
    n = B * T
    outs = _in_proj(x, mod_all, msel, W, l, tabs, T, own)
    qm, km, vm, gq, gk, gv, dq, dk, dv = [a.reshape(B, T, -1) for a in outs[:9]]
    if ctx is None:
        pm, pg, pd = [(km, vm)], [(gk, gv)], [(dk, dv)]
    else:
        ckm, cvm, cgk, cgv, cdk, cdv = ctx
        pm, pg, pd = [(ckm, cvm), (km, vm)], [(cgk, cgv), (gk, gv)], [(cdk, cdv), (dk, dv)]
    om = _attention("mla", qm, pm, []).reshape(n, -1)
    og = _attention("gqa", gq, pg, []).reshape(n, -1)
    od = _attention("diff", dq, pd, list(lam), l, lam_init).reshape(n, -1)
    x1, h2, aff = _out_proj(x, mod_all, msel, om, og, od, W, l, T)
    cap = CAPACITY_FACTOR * n // N_EXPERTS
    idx, gate = _route(aff, cap)
    tc = min(MOE_ROW_TILE, cap)
    xe = _gather(idx, h2, tc)
    ye = _ffn(xe, ew[0], ew[1], ew[2], ew[3], min(FFN_TILE, cap))
    xn = _combine(idx, gate, ye, x1, mod_all, msel, l, final_norm, tc, T, final)
    return xn, outs[9:]


def kernel(x_prompt, x_sample, cache_mla_ckv, cache_mla_kpe, cache_gqa_k, cache_gqa_v, cache_diff_k, cache_diff_v, c, c_ctx, w_ada, b_ada, norm1, norm2, w_in, mla_q_norm, mla_kv_norm, mla_w_uq, mla_w_uk, mla_w_uv, gqa_q_norm, gqa_k_norm, diff_lam_q1, diff_lam_k1, diff_lam_q2, diff_lam_k2, diff_subln, w_br_mla, w_br_gqa, w_br_diff, w_out, w_router, w_gate, w_up, w_down, final_norm):
    Bp, Tp, D = x_prompt.shape
    Bs, Ts, _ = x_sample.shape
    L = w_in.shape[0]
    P = cache_mla_ckv.shape[2]
    assert D == D_MODEL and Ts % GRID_W == 0

    nrow = -(-(1 + Bs) // 8) * 8
    cond = jnp.zeros((nrow, D), F32).at[0].set(c_ctx).at[1:1 + Bs].set(c)
    mod_all = _ada(cond, w_ada, b_ada).reshape(L, nrow, 6, D)

    tabs = _rope_tables(Ts)
    fn = final_norm[None]
    xp = x_prompt.reshape(Bp * Tp, D)
    xs = x_sample.reshape(Bs * Ts, D)
    owns = []
    W = _prep_weights(w_in, norm1, norm2, mla_q_norm, mla_kv_norm, mla_w_uq, mla_w_uk, mla_w_uv,
                      gqa_q_norm, gqa_k_norm, w_br_mla, w_br_gqa, w_br_diff, w_out, w_router)
    lam = tuple(a.reshape(L, 1, -1) for a in (diff_lam_q1, diff_lam_k1, diff_lam_q2, diff_lam_k2, diff_subln))
    kpe_pad_all = jnp.pad(cache_mla_kpe, ((0, 0), (0, 0), (0, 0), (0, LANES - MLA_ROPE)))
    for l in range(L):
        ew = (w_gate, w_up, w_down, l)
        lam_init = 0.8 - 0.6 * math.exp(-0.3 * l)
        final = l == L - 1
        xp, own = _trunk(xp, mod_all, (0, False), W, l, ew, lam, lam_init, None, None, Bp, Tp, fn, final, True)
        owns.append(own)
        kpe_pad = kpe_pad_all[:, l].reshape(Bs * P, LANES)
        ckm, cvm = _cache_proj(cache_mla_ckv[:, l].reshape(Bs * P, MLA_KV_RANK), kpe_pad, W, l)
        ctx = (ckm.reshape(Bs, P, -1), cvm.reshape(Bs, P, -1),
               cache_gqa_k[:, l].reshape(Bs, P, -1), cache_gqa_v[:, l].reshape(Bs, P, -1),
               cache_diff_k[:, l].reshape(Bs, P, -1), cache_diff_v[:, l].reshape(Bs, P, -1))
        xs, _ = _trunk(xs, mod_all, (1, True), W, l, ew, lam, lam_init, tabs, ctx, Bs, Ts, fn, final, False)

    def stack(k, shape):
        return jnp.stack([o[k].reshape((Bp, Tp) + shape) for o in owns], axis=1)

    return (xp.reshape(Bp, Tp, D), xs.reshape(Bs, Ts, D),
            stack(0, (MLA_KV_RANK,)), stack(1, (MLA_ROPE,)),
            stack(2, (GQA_KV_HEADS, GQA_DIM)), stack(3, (GQA_KV_HEADS, GQA_DIM)),
            stack(4, (DIFF_HEADS, 2 * DIFF_DIM)), stack(5, (DIFF_HEADS, DIFF_V)))
```

```python
import functools
import math

import jax
import jax.numpy as jnp
from jax import lax
from jax.experimental import pallas as pl
from jax.experimental.pallas import tpu as pltpu

F32 = jnp.float32
BF16 = jnp.bfloat16
I32 = jnp.int32

D_MODEL = 1024
GRID_W = 64
ROPE_BASE = 10000.0
EPS = 1e-6
MLA_HEADS, MLA_NOPE, MLA_ROPE, MLA_V = 8, 64, 32, 64
MLA_Q_RANK, MLA_KV_RANK = 384, 256
GQA_HEADS, GQA_KV_HEADS, GQA_DIM = 8, 2, 64
DIFF_HEADS, DIFF_DIM = 4, 64
DIFF_V = 2 * DIFF_DIM
N_EXPERTS, EXPERT_FF, CAPACITY_FACTOR = 16, 1024, 2

LANES = 128
SUBLANES = 8
HEAD_PAD = 128
ROW_TILE = 512
IN_SUB_TILE = 256
OUT_ROW_TILE = 512
OUT_SUB_TILE = 256
Q_TILE = 1024
ATTN_ROWS_PER_STEP = 1024
FFN_TILE = 1024
MOE_ROW_TILE = 512
VMEM_LIMIT = 56 * 1024 * 1024

_O_CQ, _O_CKV, _O_GQ, _O_GK, _O_GV, _O_DQ, _O_DK, _O_DV, _O_KPE, _W1_COLS = (
    0, 384, 640, 1152, 1280, 1408, 1920, 2432, 2944, 3072)
LOG2E = 1.4426950408889634


def _cparams(sem):
    return pltpu.CompilerParams(dimension_semantics=sem, vmem_limit_bytes=VMEM_LIMIT)


def _dot(a, b):
    return jnp.dot(a, b, preferred_element_type=F32)


def _dot_nt(a, b):
    return lax.dot_general(a, b, (((1,), (1,)), ((), ())), preferred_element_type=F32)


def _rms(x, g):
    return x * lax.rsqrt(jnp.mean(x * x, axis=-1, keepdims=True) + EPS) * g


def _lane_iota(shape):
    return lax.broadcasted_iota(I32, shape, len(shape) - 1)


def _rope(x, cos, sin, width):
    outs = []
    even = ((_lane_iota((1, LANES)) // width) % 2) == 0
    for j in range(x.shape[1] // LANES):
        xb = x[:, j * LANES:(j + 1) * LANES]
        partner = jnp.where(even, pltpu.roll(xb, LANES - width, 1), pltpu.roll(xb, width, 1))
        outs.append(xb * cos + partner * sin)
    return outs[0] if len(outs) == 1 else jnp.concatenate(outs, axis=1)


def _group_mean_sq(x, bd):
    sq = x * x
    hi = sq.astype(BF16)
    lo = (sq - hi.astype(F32)).astype(BF16)
    return (_dot(hi, bd) + _dot(lo, bd)) * (1.0 / GQA_DIM)


def _ada_body(c_ref, w_ref, b_ref, o_ref):
    c = c_ref[...]
    s = c / (1.0 + jnp.exp(-c))
    o_ref[0] = _dot(s.astype(BF16), w_ref[0].astype(BF16)) + b_ref[0]


def _ada(cond, w_ada, b_ada):
    L, D, N = w_ada.shape
    R = cond.shape[0]
    tn = 512
    return pl.pallas_call(
        _ada_body,
        out_shape=jax.ShapeDtypeStruct((L, R, N), F32),
        grid=(L, N // tn),
        in_specs=[pl.BlockSpec((R, D), lambda l, j: (0, 0)),
                  pl.BlockSpec((1, D, tn), lambda l, j: (l, 0, j)),
                  pl.BlockSpec((1, 1, tn), lambda l, j: (l, 0, j))],
        out_specs=pl.BlockSpec((1, R, tn), lambda l, j: (l, 0, j)),
        compiler_params=_cparams(("parallel", "parallel")),
        name="ada",
    )(cond, w_ada, b_ada.reshape(L, 1, N))


def _in_body(rope, own, n_alias, *refs):
    (x_ref, mod_ref, n1_ref, w_ref, qn_ref, kvn_ref, gqn_ref, gkn_ref, wuq_ref, wuk_ref, wuv_ref,
     bd_ref) = refs[:12]
    k = 12
    if rope:
        c64_ref, s64_ref, cm_ref, sm_ref = refs[k:k + 4]
        k += 4
    k += n_alias
    (qm_ref, km_ref, vm_ref, gq_ref, gk_ref, gv_ref, dq_ref, dk_ref, dv_ref) = refs[k:k + 9]
    k += 9
    if own:
        ockv_ref, okpe_ref, ogk_ref, ogv_ref, odk_ref, odv_ref = refs[k:k + 6]

    shift = mod_ref[0, 0:1, :]
    scale = mod_ref[0, 1:2, :]
    bd = bd_ref[...]
    sub = min(IN_SUB_TILE, x_ref.shape[0])
    for st in range(x_ref.shape[0] // sub):
        rows = pl.ds(st * sub, sub)
        x = x_ref[rows, :]
        h = _rms(x, n1_ref[...]) * (1.0 + scale) + shift
        z = _dot(h.astype(BF16), w_ref[...])

        def rot64(u):
            return _rope(u, c64_ref[rows, :], s64_ref[rows, :], 16) if rope else u

        def rotm(u):
            return _rope(u, cm_ref[rows, :], sm_ref[rows, :], 8) if rope else u

        cq = _rms(z[:, _O_CQ:_O_CKV], qn_ref[...])
        q = _dot(cq.astype(BF16), wuq_ref[...])
        qm_ref[rows, :] = (rotm(q) * (LOG2E * (MLA_NOPE + MLA_ROPE) ** -0.5)).astype(BF16)
        ckv = _rms(z[:, _O_CKV:_O_GQ], kvn_ref[...])
        kpe = z[:, _O_KPE:_W1_COLS]
        ckv_b = ckv.astype(BF16)
        kpe_at = pltpu.roll(kpe, MLA_NOPE, 1)
        knope = _dot(ckv_b, wuk_ref[...])
        kfull = knope + jnp.concatenate([kpe_at] * MLA_HEADS, axis=1)
        km_ref[rows, :] = rotm(kfull).astype(BF16)
        vm_ref[rows, :] = _dot(ckv_b, wuv_ref[...]).astype(BF16)
        gq = z[:, _O_GQ:_O_GK]
        gq = gq * lax.rsqrt(_group_mean_sq(gq, bd) + EPS) * gqn_ref[...]
        gq_ref[rows, :] = (rot64(gq) * (LOG2E * GQA_DIM ** -0.5)).astype(BF16)
        gk = z[:, _O_GK:_O_GV]
        gk = gk * lax.rsqrt(_group_mean_sq(gk, bd[:LANES, :LANES]) + EPS) * gkn_ref[...]
        gk_ref[rows, :] = rot64(gk).astype(BF16)
        gv = z[:, _O_GV:_O_DQ]
        gv_ref[rows, :] = gv.astype(BF16)
        dq_ref[rows, :] = (rot64(z[:, _O_DQ:_O_DK]) * (LOG2E * DIFF_DIM ** -0.5)).astype(BF16)
        dk = z[:, _O_DK:_O_DV]
        dk_ref[rows, :] = rot64(dk).astype(BF16)
        dv = z[:, _O_DV:_O_KPE]
        dv_ref[rows, :] = dv.astype(BF16)
        if own:
            T = ockv_ref.shape[1]
            assert sub % T == 0
            for p in range(sub // T):
                b = st * (sub // T) + p
                rl = slice(p * T, (p + 1) * T)
                ockv_ref[b] = ckv[rl]
                okpe_ref[b] = kpe[rl, :MLA_ROPE]
                for g in range(GQA_KV_HEADS):
                    ogk_ref[b, :, g, :] = gk[rl, g * GQA_DIM:(g + 1) * GQA_DIM]
                    ogv_ref[b, :, g, :] = gv[rl, g * GQA_DIM:(g + 1) * GQA_DIM]
                for hd in range(DIFF_HEADS):
                    odk_ref[b, :, hd, :] = dk[rl, hd * LANES:(hd + 1) * LANES]
                    odv_ref[b, :, hd, :] = dv[rl, hd * LANES:(hd + 1) * LANES]


def _layer_spec(a, l):
    zeros = (0,) * (a.ndim - 1)
    return pl.BlockSpec((None,) + a.shape[1:], lambda *_: (l,) + zeros)


def _mod_spec(mod_all, msel, l, tpb, tile_of=lambda i: i):
    row0, per_batch = msel
    D = mod_all.shape[-1]
    if per_batch:
        return pl.BlockSpec((None, 1, 6, D), lambda i, *_: (l, row0 + tile_of(i) // tpb, 0, 0))
    return pl.BlockSpec((None, 1, 6, D), lambda i, *_: (l, row0, 0, 0))


def _row_tile(n, T, msel, tile=ROW_TILE):
    return min(tile, T) if msel[1] else min(tile, n)


def _in_proj(x, mod_all, msel, W, l, tabs, T, own):
    n, D = x.shape
    tm = _row_tile(n, T, msel)
    rope = tabs is not None
    tpb = max(T // tm, 1)
    row = lambda i: (i, 0)
    stacked = [W[k] for k in ("norm1", "w1", "mla_q_norm", "mla_kv_norm", "gqa_q_norm", "gqa_k_norm",
                              "wuq", "wuk", "wuv")]
    ins = [x, mod_all] + stacked + [W["bd"]]
    in_specs = ([pl.BlockSpec((tm, D), row), _mod_spec(mod_all, msel, l, tpb)]
                + [_layer_spec(a, l) for a in stacked] + [pl.BlockSpec(W["bd"].shape, lambda i: (0, 0))])
    if rope:
        ins += list(tabs)
        in_specs += [pl.BlockSpec((tm, LANES), lambda i: (i % tpb, 0))] * 4
    widths = [(1024, BF16), (1024, BF16), (512, BF16), (512, BF16), (128, BF16), (128, BF16),
              (512, BF16), (512, BF16), (512, BF16)]
    out_shape = [jax.ShapeDtypeStruct((n, w), dt) for w, dt in widths]
    out_specs = [pl.BlockSpec((tm, w), row) for w, _ in widths]
    aliases = {}
    if own is not None:
        L = mod_all.shape[0]
        bpt = tm // T
        assert bpt * T == tm
        tails = [(MLA_KV_RANK,), (MLA_ROPE,), (GQA_KV_HEADS, GQA_DIM), (GQA_KV_HEADS, GQA_DIM),
                 (DIFF_HEADS, 2 * DIFF_DIM), (DIFF_HEADS, DIFF_V)]
        for k, tail in enumerate(tails):
            zeros = (0,) * (1 + len(tail))
            out_shape.append(jax.ShapeDtypeStruct((n // T, L, T) + tail, F32))
            out_specs.append(pl.BlockSpec((bpt, None, T) + tail, lambda i, zeros=zeros: (i, l) + zeros))
            if own:
                aliases[len(ins)] = len(widths) + k
                ins.append(own[k])
                in_specs.append(pl.BlockSpec(memory_space=pl.ANY))
    return pl.pallas_call(
        functools.partial(_in_body, rope, own is not None, len(own) if own else 0),
        out_shape=out_shape, grid=(n // tm,), in_specs=in_specs, out_specs=out_specs,
        input_output_aliases=aliases,
        compiler_params=_cparams(("parallel",)), name="in_proj",
    )(*ins)


def _cache_body(ckv_ref, kpe_ref, wuk_ref, wuv_ref, km_ref, vm_ref):
    ckv_b = ckv_ref[...].astype(BF16)
    kpe_at = pltpu.roll(kpe_ref[...], MLA_NOPE, 1)
    km_ref[...] = (_dot(ckv_b, wuk_ref[...]) + jnp.concatenate([kpe_at] * MLA_HEADS, axis=1)).astype(BF16)
    vm_ref[...] = _dot(ckv_b, wuv_ref[...]).astype(BF16)


def _cache_proj(ckv, kpe_pad, W, l):
    n = ckv.shape[0]
    tm = min(ROW_TILE, n)
    row = lambda i: (i, 0)
    return pl.pallas_call(
        _cache_body,
        out_shape=[jax.ShapeDtypeStruct((n, 1024), BF16), jax.ShapeDtypeStruct((n, 512), BF16)],
        grid=(n // tm,),
        in_specs=[pl.BlockSpec((tm, MLA_KV_RANK), row), pl.BlockSpec((tm, LANES), row),
                  _layer_spec(W["wuk"], l), _layer_spec(W["wuv"], l)],
        out_specs=[pl.BlockSpec((tm, 1024), row), pl.BlockSpec((tm, 512), row)],
        compiler_params=_cparams(("parallel",)), name="cache_proj",
    )(ckv, kpe_pad, W["wuk"], W["wuv"])


def _swap_halves(a):
    return pltpu.roll(a.astype(F32), LANES // 2, 1).astype(a.dtype)


def _softmax_av(q, ks, vs, v_has_ones):
    ss = [_dot_nt(q, k) for k in ks]
    mx = ss[0].max(axis=-1, keepdims=True)
    for s in ss[1:]:
        mx = jnp.maximum(mx, s.max(axis=-1, keepdims=True))
    acc, den = None, None
    for s, v in zip(ss, vs):
        p = jnp.exp2(s - mx)
        if not v_has_ones:
            d = p.sum(axis=-1, keepdims=True)
            den = d if den is None else den + d
        a = _dot(p.astype(BF16), v)
        acc = a if acc is None else acc + a
    return acc, den


def _attn_body(mode, nparts, lam_init, *refs):
    q_ref = refs[0]
    kv = [(refs[1 + 2 * i], refs[2 + 2 * i]) for i in range(nparts)]
    k = 1 + 2 * nparts
    if mode == "diff":
        lq1, lk1, lq2, lk2, sub_ref = refs[k:k + 5]
        k += 5
    o_ref = refs[k]

    lo = _lane_iota((1, LANES)) < (LANES // 2)
    hi = jnp.logical_not(lo)
    ones = jnp.ones((1, LANES), BF16)
    cols = lambda j: slice(j * LANES, (j + 1) * LANES)

    def normalise(full):
        return full / pltpu.roll(full, LANES // 2, 1)

    if mode == "diff":
        lam = (jnp.exp(jnp.sum(lq1[...] * lk1[...], axis=-1, keepdims=True))
               - jnp.exp(jnp.sum(lq2[...] * lk2[...], axis=-1, keepdims=True)) + lam_init)
    for bi in range(q_ref.shape[0]):
        if mode == "gqa":
            ks = [kr[bi].astype(BF16) for kr, _ in kv]
            vs = [vr[bi].astype(BF16) for _, vr in kv]
            ks_sw = [_swap_halves(a) for a in ks]
            vs_sw = [_swap_halves(a) for a in vs]
            for pair in range(GQA_HEADS // 2):
                qp = q_ref[bi, :, cols(pair)]
                res = []
                for L in range(2):
                    g = (2 * pair + L) // (GQA_HEADS // GQA_KV_HEADS)
                    mine = lo if L == 0 else hi
                    kk = ks if g == L else ks_sw
                    vv = [jnp.where(mine, a, ones) for a in (vs if g == L else vs_sw)]
                    full, _ = _softmax_av(jnp.where(mine, qp, jnp.zeros_like(qp)), kk, vv, True)
                    res.append(normalise(full))
                o_ref[bi, :, cols(pair)] = jnp.where(lo, res[0], res[1]).astype(o_ref.dtype)
        elif mode == "mla":
            for pair in range(MLA_HEADS // 2):
                vs = [vr[bi, :, cols(pair)] for _, vr in kv]
                res = []
                for L in range(2):
                    mine = lo if L == 0 else hi
                    kk = [kr[bi, :, cols(2 * pair + L)] for kr, _ in kv]
                    vv = [jnp.where(mine, a, ones) for a in vs]
                    full, _ = _softmax_av(q_ref[bi, :, cols(2 * pair + L)], kk, vv, True)
                    res.append(normalise(full))
                o_ref[bi, :, cols(pair)] = jnp.where(lo, res[0], res[1]).astype(o_ref.dtype)
        else:
            for h in range(DIFF_HEADS):
                qh = q_ref[bi, :, cols(h)]
                head = lambda r: (r[bi, :, h, :] if len(r.shape) == 4 else r[bi, :, cols(h)]).astype(BF16)
                kk = [head(kr) for kr, _ in kv]
                vv = [head(vr) for _, vr in kv]
                zero = jnp.zeros_like(qh)
                a1, d1 = _softmax_av(jnp.where(lo, qh, zero), kk, vv, False)
                a2, d2 = _softmax_av(jnp.where(lo, zero, qh), kk, vv, False)
                o = _rms(a1 / d1 - lam * (a2 / d2), sub_ref[...]) * (1.0 - lam_init)
                o_ref[bi, :, cols(h)] = o.astype(o_ref.dtype)


def _attention(mode, q, parts, extra, l=0, lam_init=0.0):
    B, T, C = q.shape
    tq = min(Q_TILE, T)
    per_step = {"diff": ATTN_ROWS_PER_STEP // tq, "gqa": 2, "mla": 1}[mode]
    bb = max(1, min(B, per_step)) if (T == tq and T < Q_TILE) else 1
    while B % bb:
        bb -= 1
    ins = [q]
    in_specs = [pl.BlockSpec((bb, tq, C), lambda b, i: (b, i, 0))]
    def part_spec(a):
        if a.ndim == 5:
            return pl.BlockSpec((bb, None) + a.shape[2:], lambda b, i: (b, l, 0, 0, 0))
        return pl.BlockSpec((bb,) + a.shape[1:], lambda b, i: (b, 0, 0))

    for k, v in parts:
        ins += [k, v]
        in_specs += [part_spec(k), part_spec(v)]
    for e in extra:
        ins.append(e)
        in_specs.append(_layer_spec(e, l))
    return pl.pallas_call(
        functools.partial(_attn_body, mode, len(parts), lam_init),
        out_shape=jax.ShapeDtypeStruct((B, T, 512), BF16),
        grid=(B // bb, T // tq),
        in_specs=in_specs,
        out_specs=pl.BlockSpec((bb, tq, 512), lambda b, i: (b, i, 0)),
        compiler_params=_cparams(("parallel", "parallel")), name="attn_" + mode,
    )(*ins)


def _out_body(x_ref, mod_ref, n1_ref, n2_ref, om_ref, og_ref, od_ref, wg_ref, bm_ref, bg_ref, bdf_ref,
              wo_ref, rh_ref, rl_ref, x1_ref, h2_ref, aff_ref):
    shift1, scale1, gate1 = mod_ref[0, 0:1, :], mod_ref[0, 1:2, :], mod_ref[0, 2:3, :]
    shift2, scale2 = mod_ref[0, 3:4, :], mod_ref[0, 4:5, :]
    D = x_ref.shape[1]
    sig = lambda u: 1.0 / (1.0 + jnp.exp(-u))
    valid = _lane_iota((1, LANES)) < N_EXPERTS
    sub = min(OUT_SUB_TILE, x_ref.shape[0])
    for s in range(x_ref.shape[0] // sub):
        rows = pl.ds(s * sub, sub)
        x = x_ref[rows, :]
        h = (_rms(x, n1_ref[...]) * (1.0 + scale1) + shift1).astype(BF16)
        merged = None
        for b, (o_ref, w_ref) in enumerate(((om_ref, bm_ref), (og_ref, bg_ref), (od_ref, bdf_ref))):
            t = sig(_dot(h, wg_ref[:, b * D:(b + 1) * D])) * _dot(o_ref[rows, :], w_ref[...])
            merged = t if merged is None else merged + t
        y = _dot(merged.astype(BF16), wo_ref[...])
        x1 = x + gate1 * y
        x1_ref[rows, :] = x1
        h2 = _rms(x1, n2_ref[...]) * (1.0 + scale2) + shift2
        h2_ref[rows, :] = h2
        hi = h2.astype(BF16)
        lo = (h2 - hi.astype(F32)).astype(BF16)
        logits = _dot(hi, rh_ref[...]) + _dot(lo, rh_ref[...]) + _dot(hi, rl_ref[...])
        logits = jnp.where(valid, logits, -1e30)
        e = jnp.exp(logits - logits.max(axis=-1, keepdims=True))
        aff = e / e.sum(axis=-1, keepdims=True)
        aff_ref[:, s * sub:(s + 1) * sub] = aff.T[:N_EXPERTS, :]


def _out_proj(x, mod_all, msel, om, og, od, W, l, T):
    n, D = x.shape
    tm = _row_tile(n, T, msel, OUT_ROW_TILE)
    tpb = max(T // tm, 1)
    row = lambda i: (i, 0)
    ws = [W[k] for k in ("wgate", "wbr_mla", "wbr_gqa", "wbr_diff", "w_out", "wr_hi", "wr_lo")]
    return pl.pallas_call(
        _out_body,
        out_shape=[jax.ShapeDtypeStruct((n, D), F32), jax.ShapeDtypeStruct((n, D), F32),
                   jax.ShapeDtypeStruct((N_EXPERTS, n), F32)],
        grid=(n // tm,),
        in_specs=[pl.BlockSpec((tm, D), row), _mod_spec(mod_all, msel, l, tpb), _layer_spec(W["norm1"], l),
                  _layer_spec(W["norm2"], l), pl.BlockSpec((tm, 512), row), pl.BlockSpec((tm, 512), row),
                  pl.BlockSpec((tm, 512), row)] + [_layer_spec(w, l) for w in ws],
        out_specs=[pl.BlockSpec((tm, D), row), pl.BlockSpec((tm, D), row),
                   pl.BlockSpec((N_EXPERTS, tm), lambda i: (0, i))],
        compiler_params=_cparams(("parallel",)), name="out_proj",
    )(x, mod_all, W["norm1"], W["norm2"], om, og, od, *ws)


def _route_body(cap, aff_ref, idx_ref, gate_ref):
    v = aff_ref[...]
    E, n = v.shape
    nbits = max(1, (n - 1).bit_length())
    lane = _lane_iota((E, n))

    def count(mask):
        return jnp.sum(mask.astype(I32), axis=1, keepdims=True)

    def step(i, t):
        cand = t | lax.shift_left(jnp.int32(1), 30 - i)
        return jnp.where(count(v >= lax.bitcast_convert_type(cand, F32)) >= cap, cand, t)

    thr = lax.bitcast_convert_type(lax.fori_loop(0, 31, step, jnp.zeros((E, 1), I32)), F32)

    def prefix(u):
        for b in range(nbits):
            s = 1 << b
            u = u + jnp.where(lane >= s, pltpu.roll(u, s, 1), 0)
        return u

    gt = v > thr
    eq = v == thr
    need = cap - count(gt)
    eq_i = eq.astype(I32)
    eq_rank = prefix(eq_i) - eq_i
    sel = jnp.where(gt | (eq & (eq_rank < need)), 1, 0).astype(I32)
    dist = lane + 1 - prefix(sel)
    tok = lane
    for b in range(nbits):
        s = 1 << b
        mv = sel * ((dist >> b) & 1)
        stay = sel - mv
        take = pltpu.roll(mv, n - s, 1) == 1
        tok = jnp.where(take, pltpu.roll(tok, n - s, 1), tok)
        v = jnp.where(take, pltpu.roll(v, n - s, 1), v)
        dist = jnp.where(take, pltpu.roll(dist, n - s, 1), dist)
        sel = jnp.where(take, 1, stay)
    idx_ref[...] = tok[:, :cap]
    gate_ref[...] = v[:, :cap]


def _route(aff_t, cap):
    E, n = aff_t.shape
    return pl.pallas_call(
        functools.partial(_route_body, cap),
        out_shape=[jax.ShapeDtypeStruct((E, cap), I32), jax.ShapeDtypeStruct((E, cap), F32)],
        compiler_params=pltpu.CompilerParams(vmem_limit_bytes=VMEM_LIMIT), name="route",
    )(aff_t)


def _gather_body(tc, tiles_per_e, idx_ref, xs_ref, o_ref):
    i = pl.program_id(0)
    base = i * tc

    group = 16
    D = xs_ref.shape[1]
    sub = lax.broadcasted_iota(I32, (SUBLANES, D), 0)

    def body(g, _):
        halves = []
        for hf in range(group // SUBLANES):
            acc = jnp.zeros((SUBLANES, D), F32)
            for j in range(SUBLANES):
                r = idx_ref[base + g * group + hf * SUBLANES + j]
                r8 = pl.multiple_of((r >> 3) << 3, SUBLANES)
                blk8 = pltpu.roll(xs_ref[pl.ds(r8, SUBLANES), :], (j - r) & (SUBLANES - 1), 0)
                acc = jnp.where(sub == j, blk8, acc)
            halves.append(acc)
        rows = pl.ds(pl.multiple_of(g * group, group), group)
        o_ref[rows, :] = jnp.concatenate(halves, axis=0).astype(BF16)
        return 0

    lax.fori_loop(0, tc // group, body, 0)


def _gather(idx, xs, tc):
    E, cap = idx.shape
    n, D = xs.shape
    tiles = cap // tc
    return pl.pallas_call(
        functools.partial(_gather_body, tc, tiles),
        out_shape=jax.ShapeDtypeStruct((E * cap, D), BF16),
        grid_spec=pltpu.PrefetchScalarGridSpec(
            num_scalar_prefetch=1, grid=(E * tiles,),
            in_specs=[pl.BlockSpec((n, D), lambda i, idx: (0, 0), pipeline_mode=pl.Buffered(1))],
            out_specs=pl.BlockSpec((tc, D), lambda i, idx: (i, 0))),
        compiler_params=_cparams(("arbitrary",)), name="moe_gather",
    )(idx.reshape(-1), xs)


def _ffn_body(xe_ref, wg_ref, wu_ref, wd_ref, o_ref):
    xe = xe_ref[...]
    hg = _dot(xe, wg_ref[0, 0].astype(BF16))
    hu = _dot(xe, wu_ref[0, 0].astype(BF16))
    a = (hg / (1.0 + jnp.exp(-hg))) * hu
    o_ref[...] = _dot(a.astype(BF16), wd_ref[0, 0].astype(BF16))


def _ffn(xe, wg, wu, wd, l, tc):
    _, E, D, FF = wg.shape
    rows = xe.shape[0]
    tiles = rows // E // tc
    wmap = lambda e, t: (l, e, 0, 0)
    row = lambda e, t: (e * tiles + t, 0)
    return pl.pallas_call(
        _ffn_body,
        out_shape=jax.ShapeDtypeStruct((rows, D), F32),
        grid=(E, tiles),
        in_specs=[pl.BlockSpec((tc, D), row), pl.BlockSpec((1, 1, D, FF), wmap),
                  pl.BlockSpec((1, 1, D, FF), wmap), pl.BlockSpec((1, 1, FF, D), wmap)],
        out_specs=pl.BlockSpec((tc, D), row),
        compiler_params=_cparams(("parallel", "parallel")), name="moe_ffn",
    )(xe, wg, wu, wd)


def _combine_body(tc, tiles_per_e, n_scatter, final, idx_ref, gate_ref, ye_ref, x1_ref, mod_ref, fn_ref,
                  o_ref, acc_ref):
    i = pl.program_id(0)

    @pl.when(i == 0)
    def _():
        acc_ref[...] = jnp.zeros_like(acc_ref)

    @pl.when(i < n_scatter)
    def _():
        base = i * tc

        sub = lax.broadcasted_iota(I32, (SUBLANES, LANES), 0)

        nlane = ye_ref.shape[1] // LANES
        ways = 4

        def body(gq, _):
            s0 = pl.multiple_of(gq * (ways * SUBLANES), ways * SUBLANES)
            ye8 = [ye_ref[pl.ds(s0 + w * SUBLANES, SUBLANES), :] for w in range(ways)]
            for j in range(SUBLANES):
                upd = []
                for w in range(ways):
                    r = idx_ref[base + s0 + w * SUBLANES + j]
                    g = gate_ref[base + s0 + w * SUBLANES + j]
                    r8 = pl.multiple_of((r >> 3) << 3, SUBLANES)
                    rolled = pltpu.roll(ye8[w], (r - j) & (SUBLANES - 1), 0)
                    gmask = jnp.where(sub == (r & (SUBLANES - 1)), g, 0.0)
                    upd.append((r8, acc_ref[pl.ds(r8, SUBLANES), :]
                                + rolled * jnp.concatenate([gmask] * nlane, axis=1)))
                for r8, val in upd:
                    acc_ref[pl.ds(r8, SUBLANES), :] = val
            return 0

        lax.fori_loop(0, tc // (ways * SUBLANES), body, 0)

    @pl.when(i >= n_scatter)
    def _():
        tm = x1_ref.shape[0]
        j = i - n_scatter
        rows = pl.ds(pl.multiple_of(j * tm, tm), tm)
        xn = x1_ref[...] + mod_ref[0, 5:6, :] * acc_ref[rows, :]
        o_ref[...] = _rms(xn, fn_ref[...]) if final else xn


def _combine(idx, gate, ye, x1, mod_all, msel, l, final_norm, tc, T, final):
    E, cap = idx.shape
    n, D = x1.shape
    tm = _row_tile(n, T, msel)
    tiles = cap // tc
    n_sc = E * tiles
    tpb = max(T // tm, 1)
    out_tile = lambda i: jnp.maximum(i - n_sc, 0)
    out_row = lambda i, a, b: (out_tile(i), 0)
    return pl.pallas_call(
        functools.partial(_combine_body, tc, tiles, n_sc, final),
        out_shape=jax.ShapeDtypeStruct((n, D), F32),
        grid_spec=pltpu.PrefetchScalarGridSpec(
            num_scalar_prefetch=2, grid=(n_sc + n // tm,),
            in_specs=[pl.BlockSpec((tc, D), lambda i, a, b: (jnp.minimum(i, n_sc - 1), 0)),
                      pl.BlockSpec((tm, D), out_row), _mod_spec(mod_all, msel, l, tpb, out_tile),
                      pl.BlockSpec((1, D), lambda i, a, b: (0, 0))],
            out_specs=pl.BlockSpec((tm, D), out_row),
            scratch_shapes=[pltpu.VMEM((n, D), F32)]),
        compiler_params=_cparams(("arbitrary",)), name="moe_combine",
    )(idx.reshape(-1), gate.reshape(-1), ye, x1, mod_all, final_norm)


def _rope_tables(T):
    t = jnp.arange(T)
    row = (t // GRID_W).astype(F32)
    col = (t % GRID_W).astype(F32)

    def table(head_w, lane_off, rep):
        half = head_w // 4
        freqs = ROPE_BASE ** (-jnp.arange(half, dtype=F32) / half)
        blk = []
        for pos in (row, col):
            ang = pos[:, None] * freqs[None, :]
            blk.append((jnp.cos(ang), jnp.sin(ang)))
        cos = jnp.concatenate([blk[0][0], blk[0][0], blk[1][0], blk[1][0]], axis=1)
        sin = jnp.concatenate([-blk[0][1], blk[0][1], -blk[1][1], blk[1][1]], axis=1)
        cos = jnp.concatenate([cos] * rep, axis=1)
        sin = jnp.concatenate([sin] * rep, axis=1)
        pad_r = LANES - lane_off - cos.shape[1]
        cos = jnp.pad(cos, ((0, 0), (lane_off, pad_r)), constant_values=1.0)
        sin = jnp.pad(sin, ((0, 0), (lane_off, pad_r)))
        return cos, sin

    c64, s64 = table(GQA_DIM, 0, 2)
    cm, sm = table(MLA_ROPE, MLA_NOPE, 1)
    return c64, s64, cm, sm


def _split_w_in_body(w_ref, w1_ref, wg_ref):
    w = w_ref[...]
    zeros = jnp.zeros((w.shape[0], _W1_COLS - _O_KPE - MLA_ROPE), F32)
    w1 = jnp.concatenate([w[:, :640], w[:, 672:2976], w[:, 640:672], zeros], axis=1)
    w1_ref[...] = w1.astype(BF16)
    wg_ref[...] = w[:, 2976:].astype(BF16)


def _split_w_in(w_in):
    L, D, C = w_in.shape
    tr = 256
    return pl.pallas_call(
        _split_w_in_body,
        out_shape=[jax.ShapeDtypeStruct((L, D, _W1_COLS), BF16), jax.ShapeDtypeStruct((L, D, C - 2976), BF16)],
        grid=(L, D // tr),
        in_specs=[pl.BlockSpec((None, tr, C), lambda l, i: (l, i, 0))],
        out_specs=[pl.BlockSpec((None, tr, _W1_COLS), lambda l, i: (l, i, 0)),
                   pl.BlockSpec((None, tr, C - 2976), lambda l, i: (l, i, 0))],
        compiler_params=_cparams(("parallel", "parallel")), name="split_w_in",
    )(w_in)


def _prep_weights(w_in, norm1, norm2, mla_q_norm, mla_kv_norm, mla_w_uq, mla_w_uk, mla_w_uv,
                  gqa_q_norm, gqa_k_norm, w_br_mla, w_br_gqa, w_br_diff, w_out, w_router):
    L, D = w_in.shape[0], D_MODEL
    w1, wgate = _split_w_in(w_in)
    uq = mla_w_uq.reshape(L, MLA_Q_RANK, MLA_HEADS, MLA_NOPE + MLA_ROPE)
    wuq = jnp.pad(uq, ((0, 0), (0, 0), (0, 0), (0, HEAD_PAD - MLA_NOPE - MLA_ROPE))).reshape(L, MLA_Q_RANK, -1)
    wuk = jnp.pad(mla_w_uk, ((0, 0), (0, 0), (0, 0), (0, HEAD_PAD - MLA_NOPE))).reshape(L, MLA_KV_RANK, -1)
    wuv = mla_w_uv.reshape(L, MLA_KV_RANK, MLA_HEADS * MLA_V)
    grp = jnp.arange(512) // GQA_DIM
    bd = (grp[:, None] == grp[None, :]).astype(BF16)
    wr = jnp.pad(w_router, ((0, 0), (0, 0), (0, LANES - N_EXPERTS)))
    wr_hi = wr.astype(BF16)
    wr_lo = (wr - wr_hi.astype(F32)).astype(BF16)
    vec = lambda a: a.reshape(L, 1, -1)
    return {
        "norm1": vec(norm1), "norm2": vec(norm2), "w1": w1,
        "mla_q_norm": vec(mla_q_norm), "mla_kv_norm": vec(mla_kv_norm),
        "gqa_q_norm": vec(jnp.tile(gqa_q_norm, (1, GQA_HEADS))),
        "gqa_k_norm": vec(jnp.tile(gqa_k_norm, (1, GQA_KV_HEADS))),
        "wuq": wuq.astype(BF16), "wuk": wuk.astype(BF16), "wuv": wuv.astype(BF16), "bd": bd,
        "wgate": wgate, "wbr_mla": w_br_mla.astype(BF16),
        "wbr_gqa": w_br_gqa.astype(BF16), "wbr_diff": w_br_diff.astype(BF16),
        "w_out": w_out.astype(BF16), "wr_hi": wr_hi, "wr_lo": wr_lo,
    }


def _trunk(x, mod_all, msel, W, l, ew, lam, lam_init, tabs, ctx, B, T, final_norm, final, own):
    n = B * T
    outs = _in_proj(x, mod_all, msel, W, l, tabs, T, own)
    qm, km, vm, gq, gk, gv, dq, dk, dv = [a.reshape(B, T, -1) for a in outs[:9]]
    if ctx is None:
        pm, pg, pd = [(km, vm)], [(gk, gv)], [(dk, dv)]
    else:
        ckm, cvm, cgk, cgv, cdk, cdv = ctx
        pm, pg, pd = [(ckm, cvm), (km, vm)], [(cgk, cgv), (gk, gv)], [(cdk, cdv), (dk, dv)]
    om = _attention("mla", qm, pm, []).reshape(n, -1)
    og = _attention("gqa", gq, pg, []).reshape(n, -1)
    od = _attention("diff", dq, pd, list(lam), l, lam_init).reshape(n, -1)
    x1, h2, aff = _out_proj(x, mod_all, msel, om, og, od, W, l, T)
    cap = CAPACITY_FACTOR * n // N_EXPERTS
    idx, gate = _route(aff, cap)
    tc = min(MOE_ROW_TILE, cap)
    xe = _gather(idx, h2, tc)
    ye = _ffn(xe, ew[0], ew[1], ew[2], ew[3], min(FFN_TILE, cap))
    xn = _combine(idx, gate, ye, x1, mod_all, msel, l, final_norm, tc, T, final)
    return xn, outs[9:]


def kernel(x_prompt, x_sample, cache_mla_ckv, cache_mla_kpe, cache_gqa_k, cache_gqa_v, cache_diff_k, cache_diff_v, c, c_ctx, w_ada, b_ada, norm1, norm2, w_in, mla_q_norm, mla_kv_norm, mla_w_uq, mla_w_uk, mla_w_uv, gqa_q_norm, gqa_k_norm, diff_lam_q1, diff_lam_k1, diff_lam_q2, diff_lam_k2, diff_subln, w_br_mla, w_br_gqa, w_br_diff, w_out, w_router, w_gate, w_up, w_down, final_norm):
    Bp, Tp, D = x_prompt.shape
    Bs, Ts, _ = x_sample.shape
    L = w_in.shape[0]
    P = cache_mla_ckv.shape[2]
    assert D == D_MODEL and Ts % GRID_W == 0

    nrow = -(-(1 + Bs) // 8) * 8
    cond = jnp.zeros((nrow, D), F32).at[0].set(c_ctx).at[1:1 + Bs].set(c)
    mod_all = _ada(cond, w_ada, b_ada).reshape(L, nrow, 6, D)

    tabs = _rope_tables(Ts)
    fn = final_norm[None]
    xp = x_prompt.reshape(Bp * Tp, D)
    xs = x_sample.reshape(Bs * Ts, D)
    own = ()
    W = _prep_weights(w_in, norm1, norm2, mla_q_norm, mla_kv_norm, mla_w_uq, mla_w_uk, mla_w_uv,
                      gqa_q_norm, gqa_k_norm, w_br_mla, w_br_gqa, w_br_diff, w_out, w_router)
    lam = tuple(a.reshape(L, 1, -1) for a in (diff_lam_q1, diff_lam_k1, diff_lam_q2, diff_lam_k2, diff_subln))
    kpe_pad_all = jnp.pad(cache_mla_kpe, ((0, 0), (0, 0), (0, 0), (0, LANES - MLA_ROPE)))
    for l in range(L):
        ew = (w_gate, w_up, w_down, l)
        lam_init = 0.8 - 0.6 * math.exp(-0.3 * l)
        final = l == L - 1
        xp, own = _trunk(xp, mod_all, (0, False), W, l, ew, lam, lam_init, None, None, Bp, Tp, fn, final, own)
        kpe_pad = kpe_pad_all[:, l].reshape(Bs * P, LANES)
        ckm, cvm = _cache_proj(cache_mla_ckv[:, l].reshape(Bs * P, MLA_KV_RANK), kpe_pad, W, l)
        ctx = (ckm.reshape(Bs, P, -1), cvm.reshape(Bs, P, -1),
               cache_gqa_k[:, l].reshape(Bs, P, -1), cache_gqa_v[:, l].reshape(Bs, P, -1),
               cache_diff_k, cache_diff_v)
        xs, _ = _trunk(xs, mod_all, (1, True), W, l, ew, lam, lam_init, tabs, ctx, Bs, Ts, fn, final, None)

    return (xp.reshape(Bp, Tp, D), xs.reshape(Bs, Ts, D)) + tuple(own)
```

```python
import functools
import math

import jax
import jax.numpy as jnp
from jax import lax
from jax.experimental import pallas as pl
from jax.experimental.pallas import tpu as pltpu

F32 = jnp.float32
BF16 = jnp.bfloat16
I32 = jnp.int32

D_MODEL = 1024
GRID_W = 64
ROPE_BASE = 10000.0
EPS = 1e-6
MLA_HEADS, MLA_NOPE, MLA_ROPE, MLA_V = 8, 64, 32, 64
MLA_Q_RANK, MLA_KV_RANK = 384, 256
GQA_HEADS, GQA_KV_HEADS, GQA_DIM = 8, 2, 64
DIFF_HEADS, DIFF_DIM = 4, 64
DIFF_V = 2 * DIFF_DIM
N_EXPERTS, EXPERT_FF, CAPACITY_FACTOR = 16, 1024, 2

LANES = 128
SUBLANES = 8
HEAD_PAD = 128
ROW_TILE = 512
IN_SUB_TILE = 256
OUT_ROW_TILE = 512
OUT_SUB_TILE = 256
Q_TILE = 1024
ATTN_ROWS_PER_STEP = 1024
FFN_TILE = 1024
MOE_ROW_TILE = 512
VMEM_LIMIT = 56 * 1024 * 1024

_O_CQ, _O_CKV, _O_GQ, _O_GK, _O_GV, _O_DQ, _O_DK, _O_DV, _O_KPE, _W1_COLS = (
    0, 384, 640, 1152, 1280, 1408, 1920, 2432, 2944, 3072)
LOG2E = 1.4426950408889634


def _cparams(sem):
    return pltpu.CompilerParams(dimension_semantics=sem, vmem_limit_bytes=VMEM_LIMIT)


def _dot(a, b):
    return jnp.dot(a, b, preferred_element_type=F32)


def _dot_nt(a, b):
    return lax.dot_general(a, b, (((1,), (1,)), ((), ())), preferred_element_type=F32)


def _rms(x, g):
    return x * lax.rsqrt(jnp.mean(x * x, axis=-1, keepdims=True) + EPS) * g


def _lane_iota(shape):
    return lax.broadcasted_iota(I32, shape, len(shape) - 1)


def _rope(x, cos, sin, width):
    outs = []
    even = ((_lane_iota((1, LANES)) // width) % 2) == 0
    for j in range(x.shape[1] // LANES):
        xb = x[:, j * LANES:(j + 1) * LANES]
        partner = jnp.where(even, pltpu.roll(xb, LANES - width, 1), pltpu.roll(xb, width, 1))
        outs.append(xb * cos + partner * sin)
    return outs[0] if len(outs) == 1 else jnp.concatenate(outs, axis=1)


def _group_mean_sq(x, bd):
    return _dot((x * x).astype(BF16), bd) * (1.0 / GQA_DIM)


def _ada_body(c_ref, w_ref, b_ref, o_ref):
    c = c_ref[...]
    s = c / (1.0 + jnp.exp(-c))
    o_ref[0] = _dot(s.astype(BF16), w_ref[0].astype(BF16)) + b_ref[0]


def _ada(cond, w_ada, b_ada):
    L, D, N = w_ada.shape
    R = cond.shape[0]
    tn = 512
    return pl.pallas_call(
        _ada_body,
        out_shape=jax.ShapeDtypeStruct((L, R, N), F32),
        grid=(L, N // tn),
        in_specs=[pl.BlockSpec((R, D), lambda l, j: (0, 0)),
                  pl.BlockSpec((1, D, tn), lambda l, j: (l, 0, j)),
                  pl.BlockSpec((1, 1, tn), lambda l, j: (l, 0, j))],
        out_specs=pl.BlockSpec((1, R, tn), lambda l, j: (l, 0, j)),
        compiler_params=_cparams(("parallel", "parallel")),
        name="ada",
    )(cond, w_ada, b_ada.reshape(L, 1, N))


def _in_body(rope, own, n_alias, *refs):
    (x_ref, mod_ref, n1_ref, w_ref, qn_ref, kvn_ref, gqn_ref, gkn_ref, wuq_ref, wuk_ref, wuv_ref,
     bd_ref) = refs[:12]
    k = 12
    if rope:
        c64_ref, s64_ref, cm_ref, sm_ref = refs[k:k + 4]
        k += 4
    k += n_alias
    (qm_ref, km_ref, vm_ref, gq_ref, gk_ref, gv_ref, dq_ref, dk_ref, dv_ref) = refs[k:k + 9]
    k += 9
    if own:
        ockv_ref, okpe_ref, ogk_ref, ogv_ref, odk_ref, odv_ref = refs[k:k + 6]

    shift = mod_ref[0, 0:1, :]
    scale = mod_ref[0, 1:2, :]
    bd = bd_ref[...]
    sub = min(IN_SUB_TILE, x_ref.shape[0])
    for st in range(x_ref.shape[0] // sub):
        rows = pl.ds(st * sub, sub)
        x = x_ref[rows, :]
        h = _rms(x, n1_ref[...]) * (1.0 + scale) + shift
        z = _dot(h.astype(BF16), w_ref[...])

        def rot64(u):
            return _rope(u, c64_ref[rows, :], s64_ref[rows, :], 16) if rope else u

        def rotm(u):
            return _rope(u, cm_ref[rows, :], sm_ref[rows, :], 8) if rope else u

        cq = _rms(z[:, _O_CQ:_O_CKV], qn_ref[...])
        q = _dot(cq.astype(BF16), wuq_ref[...])
        qm_ref[rows, :] = (rotm(q) * (LOG2E * (MLA_NOPE + MLA_ROPE) ** -0.5)).astype(BF16)
        ckv = _rms(z[:, _O_CKV:_O_GQ], kvn_ref[...])
        kpe = z[:, _O_KPE:_W1_COLS]
        ckv_b = ckv.astype(BF16)
        kpe_at = pltpu.roll(kpe, MLA_NOPE, 1)
        knope = _dot(ckv_b, wuk_ref[...])
        kfull = knope + jnp.concatenate([kpe_at] * MLA_HEADS, axis=1)
        km_ref[rows, :] = rotm(kfull).astype(BF16)
        vm_ref[rows, :] = _dot(ckv_b, wuv_ref[...]).astype(BF16)
        gq = z[:, _O_GQ:_O_GK]
        gq = gq * lax.rsqrt(_group_mean_sq(gq, bd) + EPS) * gqn_ref[...]
        gq_ref[rows, :] = (rot64(gq) * (LOG2E * GQA_DIM ** -0.5)).astype(BF16)
        gk = z[:, _O_GK:_O_GV]
        gk = gk * lax.rsqrt(_group_mean_sq(gk, bd[:LANES, :LANES]) + EPS) * gkn_ref[...]
        gk_ref[rows, :] = rot64(gk).astype(BF16)
        gv = z[:, _O_GV:_O_DQ]
        gv_ref[rows, :] = gv.astype(BF16)
        dq_ref[rows, :] = (rot64(z[:, _O_DQ:_O_DK]) * (LOG2E * DIFF_DIM ** -0.5)).astype(BF16)
        dk = z[:, _O_DK:_O_DV]
        dk_ref[rows, :] = rot64(dk).astype(BF16)
        dv = z[:, _O_DV:_O_KPE]
        dv_ref[rows, :] = dv.astype(BF16)
        if own:
            T = ockv_ref.shape[1]
            assert sub % T == 0
            for p in range(sub // T):
                b = st * (sub // T) + p
                rl = slice(p * T, (p + 1) * T)
                ockv_ref[b] = ckv[rl]
                okpe_ref[b] = kpe[rl, :MLA_ROPE]
                for g in range(GQA_KV_HEADS):
                    ogk_ref[b, :, g, :] = gk[rl, g * GQA_DIM:(g + 1) * GQA_DIM]
                    ogv_ref[b, :, g, :] = gv[rl, g * GQA_DIM:(g + 1) * GQA_DIM]
                for hd in range(DIFF_HEADS):
                    odk_ref[b, :, hd, :] = dk[rl, hd * LANES:(hd + 1) * LANES]
                    odv_ref[b, :, hd, :] = dv[rl, hd * LANES:(hd + 1) * LANES]


def _layer_spec(a, l):
    zeros = (0,) * (a.ndim - 1)
    return pl.BlockSpec((None,) + a.shape[1:], lambda *_: (l,) + zeros)


def _mod_spec(mod_all, msel, l, tpb, tile_of=lambda i: i):
    row0, per_batch = msel
    D = mod_all.shape[-1]
    if per_batch:
        return pl.BlockSpec((None, 1, 6, D), lambda i, *_: (l, row0 + tile_of(i) // tpb, 0, 0))
    return pl.BlockSpec((None, 1, 6, D), lambda i, *_: (l, row0, 0, 0))


def _row_tile(n, T, msel, tile=ROW_TILE):
    return min(tile, T) if msel[1] else min(tile, n)


def _in_proj(x, mod_all, msel, W, l, tabs, T, own):
    n, D = x.shape
    tm = _row_tile(n, T, msel)
    rope = tabs is not None
    tpb = max(T // tm, 1)
    row = lambda i: (i, 0)
    stacked = [W[k] for k in ("norm1", "w1", "mla_q_norm", "mla_kv_norm", "gqa_q_norm", "gqa_k_norm",
                              "wuq", "wuk", "wuv")]
    ins = [x, mod_all] + stacked + [W["bd"]]
    in_specs = ([pl.BlockSpec((tm, D), row), _mod_spec(mod_all, msel, l, tpb)]
                + [_layer_spec(a, l) for a in stacked] + [pl.BlockSpec(W["bd"].shape, lambda i: (0, 0))])
    if rope:
        ins += list(tabs)
        in_specs += [pl.BlockSpec((tm, LANES), lambda i: (i % tpb, 0))] * 4
    widths = [(1024, BF16), (1024, BF16), (512, BF16), (512, BF16), (128, BF16), (128, BF16),
              (512, BF16), (512, BF16), (512, BF16)]
    out_shape = [jax.ShapeDtypeStruct((n, w), dt) for w, dt in widths]
    out_specs = [pl.BlockSpec((tm, w), row) for w, _ in widths]
    aliases = {}
    if own is not None:
        L = mod_all.shape[0]
        bpt = tm // T
        assert bpt * T == tm
        tails = [(MLA_KV_RANK,), (MLA_ROPE,), (GQA_KV_HEADS, GQA_DIM), (GQA_KV_HEADS, GQA_DIM),
                 (DIFF_HEADS, 2 * DIFF_DIM), (DIFF_HEADS, DIFF_V)]
        for k, tail in enumerate(tails):
            zeros = (0,) * (1 + len(tail))
            out_shape.append(jax.ShapeDtypeStruct((n // T, L, T) + tail, F32))
            out_specs.append(pl.BlockSpec((bpt, None, T) + tail, lambda i, zeros=zeros: (i, l) + zeros))
            if own:
                aliases[len(ins)] = len(widths) + k
                ins.append(own[k])
                in_specs.append(pl.BlockSpec(memory_space=pl.ANY))
    return pl.pallas_call(
        functools.partial(_in_body, rope, own is not None, len(own) if own else 0),
        out_shape=out_shape, grid=(n // tm,), in_specs=in_specs, out_specs=out_specs,
        input_output_aliases=aliases,
        compiler_params=_cparams(("parallel",)), name="in_proj",
    )(*ins)


def _cache_body(ckv_ref, kpe_ref, wuk_ref, wuv_ref, km_ref, vm_ref):
    ckv_b = ckv_ref[...].astype(BF16)
    kpe_at = pltpu.roll(kpe_ref[...], MLA_NOPE, 1)
    km_ref[...] = (_dot(ckv_b, wuk_ref[...]) + jnp.concatenate([kpe_at] * MLA_HEADS, axis=1)).astype(BF16)
    vm_ref[...] = _dot(ckv_b, wuv_ref[...]).astype(BF16)


def _cache_proj(ckv, kpe_pad, W, l):
    n = ckv.shape[0]
    tm = min(ROW_TILE, n)
    row = lambda i: (i, 0)
    return pl.pallas_call(
        _cache_body,
        out_shape=[jax.ShapeDtypeStruct((n, 1024), BF16), jax.ShapeDtypeStruct((n, 512), BF16)],
        grid=(n // tm,),
        in_specs=[pl.BlockSpec((tm, MLA_KV_RANK), row), pl.BlockSpec((tm, LANES), row),
                  _layer_spec(W["wuk"], l), _layer_spec(W["wuv"], l)],
        out_specs=[pl.BlockSpec((tm, 1024), row), pl.BlockSpec((tm, 512), row)],
        compiler_params=_cparams(("parallel",)), name="cache_proj",
    )(ckv, kpe_pad, W["wuk"], W["wuv"])


def _swap_halves(a):
    return pltpu.roll(a.astype(F32), LANES // 2, 1).astype(a.dtype)


def _softmax_av(q, ks, vs, v_has_ones):
    ss = [_dot_nt(q, k) for k in ks]
    mx = ss[0].max(axis=-1, keepdims=True)
    for s in ss[1:]:
        mx = jnp.maximum(mx, s.max(axis=-1, keepdims=True))
    acc, den = None, None
    for s, v in zip(ss, vs):
        p = jnp.exp2(s - mx)
        if not v_has_ones:
            d = p.sum(axis=-1, keepdims=True)
            den = d if den is None else den + d
        a = _dot(p.astype(BF16), v)
        acc = a if acc is None else acc + a
    return acc, den


def _attn_body(mode, nparts, lam_init, *refs):
    q_ref = refs[0]
    kv = [(refs[1 + 2 * i], refs[2 + 2 * i]) for i in range(nparts)]
    k = 1 + 2 * nparts
    if mode == "diff":
        lq1, lk1, lq2, lk2, sub_ref = refs[k:k + 5]
        k += 5
    o_ref = refs[k]

    lo = _lane_iota((1, LANES)) < (LANES // 2)
    hi = jnp.logical_not(lo)
    ones = jnp.ones((1, LANES), BF16)
    cols = lambda j: slice(j * LANES, (j + 1) * LANES)

    def normalise(full):
        return full / pltpu.roll(full, LANES // 2, 1)

    if mode == "diff":
        lam = (jnp.exp(jnp.sum(lq1[...] * lk1[...], axis=-1, keepdims=True))
               - jnp.exp(jnp.sum(lq2[...] * lk2[...], axis=-1, keepdims=True)) + lam_init)
    for bi in range(q_ref.shape[0]):
        if mode == "gqa":
            ks = [kr[bi].astype(BF16) for kr, _ in kv]
            vs = [vr[bi].astype(BF16) for _, vr in kv]
            ks_sw = [_swap_halves(a) for a in ks]
            vs_sw = [_swap_halves(a) for a in vs]
            for pair in range(GQA_HEADS // 2):
                qp = q_ref[bi, :, cols(pair)]
                res = []
                for L in range(2):
                    g = (2 * pair + L) // (GQA_HEADS // GQA_KV_HEADS)
                    mine = lo if L == 0 else hi
                    kk = ks if g == L else ks_sw
                    vv = [jnp.where(mine, a, ones) for a in (vs if g == L else vs_sw)]
                    full, _ = _softmax_av(jnp.where(mine, qp, jnp.zeros_like(qp)), kk, vv, True)
                    res.append(normalise(full))
                o_ref[bi, :, cols(pair)] = jnp.where(lo, res[0], res[1]).astype(o_ref.dtype)
        elif mode == "mla":
            for pair in range(MLA_HEADS // 2):
                vs = [vr[bi, :, cols(pair)] for _, vr in kv]
                res = []
                for L in range(2):
                    mine = lo if L == 0 else hi
                    kk = [kr[bi, :, cols(2 * pair + L)] for kr, _ in kv]
                    vv = [jnp.where(mine, a, ones) for a in vs]
                    full, _ = _softmax_av(q_ref[bi, :, cols(2 * pair + L)], kk, vv, True)
                    res.append(normalise(full))
                o_ref[bi, :, cols(pair)] = jnp.where(lo, res[0], res[1]).astype(o_ref.dtype)
        else:
            for h in range(DIFF_HEADS):
                qh = q_ref[bi, :, cols(h)]
                head = lambda r: (r[bi, :, h, :] if len(r.shape) == 4 else r[bi, :, cols(h)]).astype(BF16)
                kk = [head(kr) for kr, _ in kv]
                vv = [head(vr) for _, vr in kv]
                zero = jnp.zeros_like(qh)
                a1, d1 = _softmax_av(jnp.where(lo, qh, zero), kk, vv, False)
                a2, d2 = _softmax_av(jnp.where(lo, zero, qh), kk, vv, False)
                o = _rms(a1 / d1 - lam * (a2 / d2), sub_ref[...]) * (1.0 - lam_init)
                o_ref[bi, :, cols(h)] = o.astype(o_ref.dtype)


def _attention(mode, q, parts, extra, l=0, lam_init=0.0):
    B, T, C = q.shape
    tq = min(Q_TILE, T)
    per_step = {"diff": ATTN_ROWS_PER_STEP // tq, "gqa": 2, "mla": 1}[mode]
    bb = max(1, min(B, per_step)) if (T == tq and T < Q_TILE) else 1
    while B % bb:
        bb -= 1
    ins = [q]
    in_specs = [pl.BlockSpec((bb, tq, C), lambda b, i: (b, i, 0))]
    def part_spec(a):
        if a.ndim == 5:
            return pl.BlockSpec((bb, None) + a.shape[2:], lambda b, i: (b, l, 0, 0, 0))
        return pl.BlockSpec((bb,) + a.shape[1:], lambda b, i: (b, 0, 0))

    for k, v in parts:
        ins += [k, v]
        in_specs += [part_spec(k), part_spec(v)]
    for e in extra:
        ins.append(e)
        in_specs.append(_layer_spec(e, l))
    return pl.pallas_call(
        functools.partial(_attn_body, mode, len(parts), lam_init),
        out_shape=jax.ShapeDtypeStruct((B, T, 512), BF16),
        grid=(B // bb, T // tq),
        in_specs=in_specs,
        out_specs=pl.BlockSpec((bb, tq, 512), lambda b, i: (b, i, 0)),
        compiler_params=_cparams(("parallel", "parallel")), name="attn_" + mode,
    )(*ins)


def _out_body(x_ref, mod_ref, n1_ref, n2_ref, om_ref, og_ref, od_ref, wg_ref, bm_ref, bg_ref, bdf_ref,
              wo_ref, rh_ref, rl_ref, x1_ref, h2_ref, aff_ref):
    shift1, scale1, gate1 = mod_ref[0, 0:1, :], mod_ref[0, 1:2, :], mod_ref[0, 2:3, :]
    shift2, scale2 = mod_ref[0, 3:4, :], mod_ref[0, 4:5, :]
    D = x_ref.shape[1]
    sig = lambda u: 1.0 / (1.0 + jnp.exp(-u))
    valid = _lane_iota((1, LANES)) < N_EXPERTS
    sub = min(OUT_SUB_TILE, x_ref.shape[0])
    for s in range(x_ref.shape[0] // sub):
        rows = pl.ds(s * sub, sub)
        x = x_ref[rows, :]
        h = (_rms(x, n1_ref[...]) * (1.0 + scale1) + shift1).astype(BF16)
        merged = None
        for b, (o_ref, w_ref) in enumerate(((om_ref, bm_ref), (og_ref, bg_ref), (od_ref, bdf_ref))):
            t = sig(_dot(h, wg_ref[:, b * D:(b + 1) * D])) * _dot(o_ref[rows, :], w_ref[...])
            merged = t if merged is None else merged + t
        y = _dot(merged.astype(BF16), wo_ref[...])
        x1 = x + gate1 * y
        x1_ref[rows, :] = x1
        h2 = _rms(x1, n2_ref[...]) * (1.0 + scale2) + shift2
        h2_ref[rows, :] = h2
        hi = h2.astype(BF16)
        lo = (h2 - hi.astype(F32)).astype(BF16)
        logits = _dot(hi, rh_ref[...]) + _dot(lo, rh_ref[...]) + _dot(hi, rl_ref[...])
        logits = jnp.where(valid, logits, -1e30)
        e = jnp.exp(logits - logits.max(axis=-1, keepdims=True))
        aff = e / e.sum(axis=-1, keepdims=True)
        aff_ref[:, s * sub:(s + 1) * sub] = aff.T[:N_EXPERTS, :]


def _out_proj(x, mod_all, msel, om, og, od, W, l, T):
    n, D = x.shape
    tm = _row_tile(n, T, msel, OUT_ROW_TILE)
    tpb = max(T // tm, 1)
    row = lambda i: (i, 0)
    ws = [W[k] for k in ("wgate", "wbr_mla", "wbr_gqa", "wbr_diff", "w_out", "wr_hi", "wr_lo")]
    return pl.pallas_call(
        _out_body,
        out_shape=[jax.ShapeDtypeStruct((n, D), F32), jax.ShapeDtypeStruct((n, D), F32),
                   jax.ShapeDtypeStruct((N_EXPERTS, n), F32)],
        grid=(n // tm,),
        in_specs=[pl.BlockSpec((tm, D), row), _mod_spec(mod_all, msel, l, tpb), _layer_spec(W["norm1"], l),
                  _layer_spec(W["norm2"], l), pl.BlockSpec((tm, 512), row), pl.BlockSpec((tm, 512), row),
                  pl.BlockSpec((tm, 512), row)] + [_layer_spec(w, l) for w in ws],
        out_specs=[pl.BlockSpec((tm, D), row), pl.BlockSpec((tm, D), row),
                   pl.BlockSpec((N_EXPERTS, tm), lambda i: (0, i))],
        compiler_params=_cparams(("parallel",)), name="out_proj",
    )(x, mod_all, W["norm1"], W["norm2"], om, og, od, *ws)


def _route_body(cap, aff_ref, idx_ref, gate_ref, blk_ref, rot_ref):
    v = aff_ref[...]
    E, n = v.shape
    nbits = max(1, (n - 1).bit_length())
    lane = _lane_iota((E, n))

    def count(mask):
        return jnp.sum(mask.astype(I32), axis=1, keepdims=True)

    def step(i, t):
        cand = t | lax.shift_left(jnp.int32(1), 30 - i)
        return jnp.where(count(v >= lax.bitcast_convert_type(cand, F32)) >= cap, cand, t)

    thr = lax.bitcast_convert_type(lax.fori_loop(0, 31, step, jnp.zeros((E, 1), I32)), F32)

    def prefix(u):
        for b in range(nbits):
            s = 1 << b
            u = u + jnp.where(lane >= s, pltpu.roll(u, s, 1), 0)
        return u

    gt = v > thr
    eq = v == thr
    need = cap - count(gt)
    eq_i = eq.astype(I32)
    eq_rank = prefix(eq_i) - eq_i
    sel = jnp.where(gt | (eq & (eq_rank < need)), 1, 0).astype(I32)
    dist = lane + 1 - prefix(sel)
    tok = lane
    for b in range(nbits):
        s = 1 << b
        mv = sel * ((dist >> b) & 1)
        stay = sel - mv
        take = pltpu.roll(mv, n - s, 1) == 1
        tok = jnp.where(take, pltpu.roll(tok, n - s, 1), tok)
        v = jnp.where(take, pltpu.roll(v, n - s, 1), v)
        dist = jnp.where(take, pltpu.roll(dist, n - s, 1), dist)
        sel = jnp.where(take, 1, stay)
    tok = tok[:, :cap]
    idx_ref[...] = tok
    gate_ref[...] = v[:, :cap]
    blk_ref[...] = tok >> 3
    rot_ref[...] = (_lane_iota((E, cap)) - tok) & (SUBLANES - 1)


def _route(aff_t, cap):
    E, n = aff_t.shape
    return pl.pallas_call(
        functools.partial(_route_body, cap),
        out_shape=[jax.ShapeDtypeStruct((E, cap), I32), jax.ShapeDtypeStruct((E, cap), F32),
                   jax.ShapeDtypeStruct((E, cap), I32), jax.ShapeDtypeStruct((E, cap), I32)],
        compiler_params=pltpu.CompilerParams(vmem_limit_bytes=VMEM_LIMIT), name="route",
    )(aff_t)


def _gather_body(tc, blk_ref, rot_ref, xs_ref, o_ref):
    i = pl.program_id(0)
    base = i * tc

    group = 16
    D = xs_ref.shape[2]
    sub = lax.broadcasted_iota(I32, (SUBLANES, D), 0)

    def body(g, _):
        halves = []
        for hf in range(group // SUBLANES):
            acc = jnp.zeros((SUBLANES, D), F32)
            for j in range(SUBLANES):
                s = base + g * group + hf * SUBLANES + j
                blk8 = pltpu.roll(xs_ref[blk_ref[s]], rot_ref[s], 0)
                acc = jnp.where(sub == j, blk8, acc)
            halves.append(acc)
        rows = pl.ds(pl.multiple_of(g * group, group), group)
        o_ref[rows, :] = jnp.concatenate(halves, axis=0).astype(BF16)
        return 0

    lax.fori_loop(0, tc // group, body, 0)


def _gather(blk, rot, xs, tc):
    E, cap = blk.shape
    n, D = xs.shape
    tiles = cap // tc
    return pl.pallas_call(
        functools.partial(_gather_body, tc),
        out_shape=jax.ShapeDtypeStruct((E * cap, D), BF16),
        grid_spec=pltpu.PrefetchScalarGridSpec(
            num_scalar_prefetch=2, grid=(E * tiles,),
            in_specs=[pl.BlockSpec((n // SUBLANES, SUBLANES, D), lambda i, b, r: (0, 0, 0),
                                   pipeline_mode=pl.Buffered(1))],
            out_specs=pl.BlockSpec((tc, D), lambda i, b, r: (i, 0))),
        compiler_params=_cparams(("arbitrary",)), name="moe_gather",
    )(blk.reshape(-1), rot.reshape(-1), xs.reshape(n // SUBLANES, SUBLANES, D))


def _ffn_body(xe_ref, wg_ref, wu_ref, wd_ref, o_ref):
    xe = xe_ref[...]
    hg = _dot(xe, wg_ref[0, 0].astype(BF16))
    hu = _dot(xe, wu_ref[0, 0].astype(BF16))
    a = (hg / (1.0 + jnp.exp(-hg))) * hu
    o_ref[...] = _dot(a.astype(BF16), wd_ref[0, 0].astype(BF16))


def _ffn(xe, wg, wu, wd, l, tc):
    _, E, D, FF = wg.shape
    rows = xe.shape[0]
    tiles = rows // E // tc
    wmap = lambda e, t: (l, e, 0, 0)
    row = lambda e, t: (e * tiles + t, 0)
    return pl.pallas_call(
        _ffn_body,
        out_shape=jax.ShapeDtypeStruct((rows, D), F32),
        grid=(E, tiles),
        in_specs=[pl.BlockSpec((tc, D), row), pl.BlockSpec((1, 1, D, FF), wmap),
                  pl.BlockSpec((1, 1, D, FF), wmap), pl.BlockSpec((1, 1, FF, D), wmap)],
        out_specs=pl.BlockSpec((tc, D), row),
        compiler_params=_cparams(("parallel", "parallel")), name="moe_ffn",
    )(xe, wg, wu, wd)


def _combine_body(tc, tiles_per_e, n_scatter, final, idx_ref, gate_ref, ye_ref, x1_ref, mod_ref, fn_ref,
                  o_ref, acc_ref):
    i = pl.program_id(0)

    @pl.when(i == 0)
    def _():
        acc_ref[...] = jnp.zeros_like(acc_ref)

    @pl.when(i < n_scatter)
    def _():
        base = i * tc

        sub = lax.broadcasted_iota(I32, (SUBLANES, LANES), 0)

        nlane = ye_ref.shape[1] // LANES
        ways = 4

        def body(gq, _):
            s0 = pl.multiple_of(gq * (ways * SUBLANES), ways * SUBLANES)
            ye8 = [ye_ref[pl.ds(s0 + w * SUBLANES, SUBLANES), :] for w in range(ways)]
            for j in range(SUBLANES):
                upd = []
                for w in range(ways):
                    r = idx_ref[base + s0 + w * SUBLANES + j]
                    g = gate_ref[base + s0 + w * SUBLANES + j]
                    r8 = pl.multiple_of((r >> 3) << 3, SUBLANES)
                    rolled = pltpu.roll(ye8[w], (r - j) & (SUBLANES - 1), 0)
                    gmask = jnp.where(sub == (r & (SUBLANES - 1)), g, 0.0)
                    upd.append((r8, acc_ref[pl.ds(r8, SUBLANES), :]
                                + rolled * jnp.concatenate([gmask] * nlane, axis=1)))
                for r8, val in upd:
                    acc_ref[pl.ds(r8, SUBLANES), :] = val
            return 0

        lax.fori_loop(0, tc // (ways * SUBLANES), body, 0)

    @pl.when(i >= n_scatter)
    def _():
        tm = x1_ref.shape[0]
        j = i - n_scatter
        rows = pl.ds(pl.multiple_of(j * tm, tm), tm)
        xn = x1_ref[...] + mod_ref[0, 5:6, :] * acc_ref[rows, :]
        o_ref[...] = _rms(xn, fn_ref[...]) if final else xn


def _combine(idx, gate, ye, x1, mod_all, msel, l, final_norm, tc, T, final):
    E, cap = idx.shape
    n, D = x1.shape
    tm = _row_tile(n, T, msel)
    tiles = cap // tc
    n_sc = E * tiles
    tpb = max(T // tm, 1)
    out_tile = lambda i: jnp.maximum(i - n_sc, 0)
    out_row = lambda i, a, b: (out_tile(i), 0)
    return pl.pallas_call(
        functools.partial(_combine_body, tc, tiles, n_sc, final),
        out_shape=jax.ShapeDtypeStruct((n, D), F32),
        grid_spec=pltpu.PrefetchScalarGridSpec(
            num_scalar_prefetch=2, grid=(n_sc + n // tm,),
            in_specs=[pl.BlockSpec((tc, D), lambda i, a, b: (jnp.minimum(i, n_sc - 1), 0)),
                      pl.BlockSpec((tm, D), out_row), _mod_spec(mod_all, msel, l, tpb, out_tile),
                      pl.BlockSpec((1, D), lambda i, a, b: (0, 0))],
            out_specs=pl.BlockSpec((tm, D), out_row),
            scratch_shapes=[pltpu.VMEM((n, D), F32)]),
        compiler_params=_cparams(("arbitrary",)), name="moe_combine",
    )(idx.reshape(-1), gate.reshape(-1), ye, x1, mod_all, final_norm)


def _rope_tables(T):
    t = jnp.arange(T)
    row = (t // GRID_W).astype(F32)
    col = (t % GRID_W).astype(F32)

    def table(head_w, lane_off, rep):
        half = head_w // 4
        freqs = ROPE_BASE ** (-jnp.arange(half, dtype=F32) / half)
        blk = []
        for pos in (row, col):
            ang = pos[:, None] * freqs[None, :]
            blk.append((jnp.cos(ang), jnp.sin(ang)))
        cos = jnp.concatenate([blk[0][0], blk[0][0], blk[1][0], blk[1][0]], axis=1)
        sin = jnp.concatenate([-blk[0][1], blk[0][1], -blk[1][1], blk[1][1]], axis=1)
        cos = jnp.concatenate([cos] * rep, axis=1)
        sin = jnp.concatenate([sin] * rep, axis=1)
        pad_r = LANES - lane_off - cos.shape[1]
        cos = jnp.pad(cos, ((0, 0), (lane_off, pad_r)), constant_values=1.0)
        sin = jnp.pad(sin, ((0, 0), (lane_off, pad_r)))
        return cos, sin

    c64, s64 = table(GQA_DIM, 0, 2)
    cm, sm = table(MLA_ROPE, MLA_NOPE, 1)
    return c64, s64, cm, sm


def _reorder_w_in_body(wa_ref, wm_ref, w1_ref):
    wm = wm_ref[...]
    zeros = jnp.zeros((wm.shape[0], _W1_COLS - _O_KPE - MLA_ROPE), F32)
    w1 = jnp.concatenate([wa_ref[...], wm[:, MLA_ROPE:_O_KPE - 640 + MLA_ROPE], wm[:, :MLA_ROPE], zeros], axis=1)
    w1_ref[...] = w1.astype(BF16)


def _reorder_w_in(w_in):
    L, D, _ = w_in.shape
    tr = 256
    wa, wm = w_in[:, :, :640], w_in[:, :, 640:_W1_COLS]
    return pl.pallas_call(
        _reorder_w_in_body,
        out_shape=jax.ShapeDtypeStruct((L, D, _W1_COLS), BF16),
        grid=(L, D // tr),
        in_specs=[pl.BlockSpec((None, tr, wa.shape[2]), lambda l, i: (l, i, 0)),
                  pl.BlockSpec((None, tr, wm.shape[2]), lambda l, i: (l, i, 0))],
        out_specs=pl.BlockSpec((None, tr, _W1_COLS), lambda l, i: (l, i, 0)),
        compiler_params=_cparams(("parallel", "parallel")), name="reorder_w_in",
    )(wa, wm)


def _prep_weights(w_in, norm1, norm2, mla_q_norm, mla_kv_norm, mla_w_uq, mla_w_uk, mla_w_uv,
                  gqa_q_norm, gqa_k_norm, w_br_mla, w_br_gqa, w_br_diff, w_out, w_router):
    L, D = w_in.shape[0], D_MODEL
    w1 = _reorder_w_in(w_in)
    wgate = w_in[:, :, 2976:].astype(BF16)
    uq = mla_w_uq.reshape(L, MLA_Q_RANK, MLA_HEADS, MLA_NOPE + MLA_ROPE)
    wuq = jnp.pad(uq, ((0, 0), (0, 0), (0, 0), (0, HEAD_PAD - MLA_NOPE - MLA_ROPE))).reshape(L, MLA_Q_RANK, -1)
    wuk = jnp.pad(mla_w_uk, ((0, 0), (0, 0), (0, 0), (0, HEAD_PAD - MLA_NOPE))).reshape(L, MLA_KV_RANK, -1)
    wuv = mla_w_uv.reshape(L, MLA_KV_RANK, MLA_HEADS * MLA_V)
    grp = jnp.arange(512) // GQA_DIM
    bd = (grp[:, None] == grp[None, :]).astype(BF16)
    wr = jnp.pad(w_router, ((0, 0), (0, 0), (0, LANES - N_EXPERTS)))
    wr_hi = wr.astype(BF16)
    wr_lo = (wr - wr_hi.astype(F32)).astype(BF16)
    vec = lambda a: a.reshape(L, 1, -1)
    return {
        "norm1": vec(norm1), "norm2": vec(norm2), "w1": w1,
        "mla_q_norm": vec(mla_q_norm), "mla_kv_norm": vec(mla_kv_norm),
        "gqa_q_norm": vec(jnp.tile(gqa_q_norm, (1, GQA_HEADS))),
        "gqa_k_norm": vec(jnp.tile(gqa_k_norm, (1, GQA_KV_HEADS))),
        "wuq": wuq.astype(BF16), "wuk": wuk.astype(BF16), "wuv": wuv.astype(BF16), "bd": bd,
        "wgate": wgate, "wbr_mla": w_br_mla.astype(BF16),
        "wbr_gqa": w_br_gqa.astype(BF16), "wbr_diff": w_br_diff.astype(BF16),
        "w_out": w_out.astype(BF16), "wr_hi": wr_hi, "wr_lo": wr_lo,
    }


def _trunk(x, mod_all, msel, W, l, ew, lam, lam_init, tabs, ctx, B, T, final_norm, final, own):
    n = B * T
    outs = _in_proj(x, mod_all, msel, W, l, tabs, T, own)
    qm, km, vm, gq, gk, gv, dq, dk, dv = [a.reshape(B, T, -1) for a in outs[:9]]
    if ctx is None:
        pm, pg, pd = [(km, vm)], [(gk, gv)], [(dk, dv)]
    else:
        ckm, cvm, cgk, cgv, cdk, cdv = ctx
        pm, pg, pd = [(ckm, cvm), (km, vm)], [(cgk, cgv), (gk, gv)], [(cdk, cdv), (dk, dv)]
    om = _attention("mla", qm, pm, []).reshape(n, -1)
    og = _attention("gqa", gq, pg, []).reshape(n, -1)
    od = _attention("diff", dq, pd, list(lam), l, lam_init).reshape(n, -1)
    x1, h2, aff = _out_proj(x, mod_all, msel, om, og, od, W, l, T)
    cap = CAPACITY_FACTOR * n // N_EXPERTS
    idx, gate, blk, rot = _route(aff, cap)
    tc = min(MOE_ROW_TILE, cap)
    xe = _gather(blk, rot, h2, tc)
    ye = _ffn(xe, ew[0], ew[1], ew[2], ew[3], min(FFN_TILE, cap))
    xn = _combine(idx, gate, ye, x1, mod_all, msel, l, final_norm, tc, T, final)
    return xn, outs[9:]


def kernel(x_prompt, x_sample, cache_mla_ckv, cache_mla_kpe, cache_gqa_k, cache_gqa_v, cache_diff_k, cache_diff_v, c, c_ctx, w_ada, b_ada, norm1, norm2, w_in, mla_q_norm, mla_kv_norm, mla_w_uq, mla_w_uk, mla_w_uv, gqa_q_norm, gqa_k_norm, diff_lam_q1, diff_lam_k1, diff_lam_q2, diff_lam_k2, diff_subln, w_br_mla, w_br_gqa, w_br_diff, w_out, w_router, w_gate, w_up, w_down, final_norm):
    Bp, Tp, D = x_prompt.shape
    Bs, Ts, _ = x_sample.shape
    L = w_in.shape[0]
    P = cache_mla_ckv.shape[2]
    assert D == D_MODEL and Ts % GRID_W == 0

    nrow = -(-(1 + Bs) // 8) * 8
    cond = jnp.zeros((nrow, D), F32).at[0].set(c_ctx).at[1:1 + Bs].set(c)
    mod_all = _ada(cond, w_ada, b_ada).reshape(L, nrow, 6, D)

    tabs = _rope_tables(Ts)
    fn = final_norm[None]
    xp = x_prompt.reshape(Bp * Tp, D)
    xs = x_sample.reshape(Bs * Ts, D)
    own = ()
    W = _prep_weights(w_in, norm1, norm2, mla_q_norm, mla_kv_norm, mla_w_uq, mla_w_uk, mla_w_uv,
                      gqa_q_norm, gqa_k_norm, w_br_mla, w_br_gqa, w_br_diff, w_out, w_router)
    lam = tuple(a.reshape(L, 1, -1) for a in (diff_lam_q1, diff_lam_k1, diff_lam_q2, diff_lam_k2, diff_subln))
    kpe_pad_all = jnp.pad(cache_mla_kpe, ((0, 0), (0, 0), (0, 0), (0, LANES - MLA_ROPE)))
    for l in range(L):
        ew = (w_gate, w_up, w_down, l)
        lam_init = 0.8 - 0.6 * math.exp(-0.3 * l)
        final = l == L - 1
        xp, own = _trunk(xp, mod_all, (0, False), W, l, ew, lam, lam_init, None, None, Bp, Tp, fn, final, own)
        kpe_pad = kpe_pad_all[:, l].reshape(Bs * P, LANES)
        ckm, cvm = _cache_proj(cache_mla_ckv[:, l].reshape(Bs * P, MLA_KV_RANK), kpe_pad, W, l)
        ctx = (ckm.reshape(Bs, P, -1), cvm.reshape(Bs, P, -1),
               cache_gqa_k[:, l].reshape(Bs, P, -1), cache_gqa_v[:, l].reshape(Bs, P, -1),
               cache_diff_k, cache_diff_v)
        xs, _ = _trunk(xs, mod_all, (1, True), W, l, ew, lam, lam_init, tabs, ctx, Bs, Ts, fn, final, None)

    return (xp.reshape(Bp, Tp, D), xs.reshape(Bs, Ts, D)) + tuple(own)
```

```python
import functools
import math

import jax
import jax.numpy as jnp
from jax import lax
from jax.experimental import pallas as pl
from jax.experimental.pallas import tpu as pltpu

F32 = jnp.float32
BF16 = jnp.bfloat16
I32 = jnp.int32

D_MODEL = 1024
GRID_W = 64
ROPE_BASE = 10000.0
EPS = 1e-6
MLA_HEADS, MLA_NOPE, MLA_ROPE, MLA_V = 8, 64, 32, 64
MLA_Q_RANK, MLA_KV_RANK = 384, 256
GQA_HEADS, GQA_KV_HEADS, GQA_DIM = 8, 2, 64
DIFF_HEADS, DIFF_DIM = 4, 64
DIFF_V = 2 * DIFF_DIM
N_EXPERTS, EXPERT_FF, CAPACITY_FACTOR = 16, 1024, 2

LANES = 128
SUBLANES = 8
HEAD_PAD = 128
ROW_TILE = 512
IN_SUB_TILE = 256
OUT_ROW_TILE = 512
OUT_SUB_TILE = 256
Q_TILE = 1024
ATTN_ROWS_PER_STEP = 1024
FFN_TILE = 1024
MOE_ROW_TILE = 512
VMEM_LIMIT = 56 * 1024 * 1024

_O_CQ, _O_CKV, _O_GQ, _O_GK, _O_GV, _O_DQ, _O_DK, _O_DV, _O_KPE, _W1_COLS = (
    0, 384, 640, 1152, 1280, 1408, 1920, 2432, 2944, 3072)
LOG2E = 1.4426950408889634


def _cparams(sem):
    return pltpu.CompilerParams(dimension_semantics=sem, vmem_limit_bytes=VMEM_LIMIT)


def _dot(a, b):
    return jnp.dot(a, b, preferred_element_type=F32)


def _dot_nt(a, b):
    return lax.dot_general(a, b, (((1,), (1,)), ((), ())), preferred_element_type=F32)


def _rms(x, g):
    return x * lax.rsqrt(jnp.mean(x * x, axis=-1, keepdims=True) + EPS) * g


def _lane_iota(shape):
    return lax.broadcasted_iota(I32, shape, len(shape) - 1)


def _rope(x, cos, sin, width):
    outs = []
    even = ((_lane_iota((1, LANES)) // width) % 2) == 0
    for j in range(x.shape[1] // LANES):
        xb = x[:, j * LANES:(j + 1) * LANES]
        partner = jnp.where(even, pltpu.roll(xb, LANES - width, 1), pltpu.roll(xb, width, 1))
        outs.append(xb * cos + partner * sin)
    return outs[0] if len(outs) == 1 else jnp.concatenate(outs, axis=1)


def _group_mean_sq(x, bd):
    return _dot((x * x).astype(BF16), bd) * (1.0 / GQA_DIM)


def _ada_body(c_ref, w_ref, b_ref, o_ref):
    c = c_ref[...]
    s = c / (1.0 + jnp.exp(-c))
    o_ref[0] = _dot(s.astype(BF16), w_ref[0].astype(BF16)) + b_ref[0]


def _ada(cond, w_ada, b_ada):
    L, D, N = w_ada.shape
    R = cond.shape[0]
    tn = 512
    return pl.pallas_call(
        _ada_body,
        out_shape=jax.ShapeDtypeStruct((L, R, N), F32),
        grid=(L, N // tn),
        in_specs=[pl.BlockSpec((R, D), lambda l, j: (0, 0)),
                  pl.BlockSpec((1, D, tn), lambda l, j: (l, 0, j)),
                  pl.BlockSpec((1, 1, tn), lambda l, j: (l, 0, j))],
        out_specs=pl.BlockSpec((1, R, tn), lambda l, j: (l, 0, j)),
        compiler_params=_cparams(("parallel", "parallel")),
        name="ada",
    )(cond, w_ada, b_ada.reshape(L, 1, N))


def _in_body(rope, own, n_alias, *refs):
    (x_ref, mod_ref, n1_ref, w_ref, qn_ref, kvn_ref, gqn_ref, gkn_ref, wuq_ref, wuk_ref, wuv_ref,
     bd_ref) = refs[:12]
    k = 12
    if rope:
        c64_ref, s64_ref, cm_ref, sm_ref = refs[k:k + 4]
        k += 4
    k += n_alias
    (qm_ref, km_ref, vm_ref, gq_ref, gk_ref, gv_ref, dq_ref, dk_ref, dv_ref) = refs[k:k + 9]
    k += 9
    if own:
        ockv_ref, okpe_ref, ogk_ref, ogv_ref, odk_ref, odv_ref = refs[k:k + 6]

    shift = mod_ref[0, 0:1, :]
    scale = mod_ref[0, 1:2, :]
    bd = bd_ref[...]
    sub = min(IN_SUB_TILE, x_ref.shape[0])
    for st in range(x_ref.shape[0] // sub):
        rows = pl.ds(st * sub, sub)
        x = x_ref[rows, :]
        h = _rms(x, n1_ref[...]) * (1.0 + scale) + shift
        z = _dot(h.astype(BF16), w_ref[...])

        def rot64(u):
            return _rope(u, c64_ref[rows, :], s64_ref[rows, :], 16) if rope else u

        def rotm(u):
            return _rope(u, cm_ref[rows, :], sm_ref[rows, :], 8) if rope else u

        cq = _rms(z[:, _O_CQ:_O_CKV], qn_ref[...])
        q = _dot(cq.astype(BF16), wuq_ref[...])
        qm_ref[rows, :] = (rotm(q) * (LOG2E * (MLA_NOPE + MLA_ROPE) ** -0.5)).astype(BF16)
        ckv = _rms(z[:, _O_CKV:_O_GQ], kvn_ref[...])
        kpe = z[:, _O_KPE:_W1_COLS]
        ckv_b = ckv.astype(BF16)
        kpe_at = pltpu.roll(kpe, MLA_NOPE, 1)
        knope = _dot(ckv_b, wuk_ref[...])
        kfull = knope + jnp.concatenate([kpe_at] * MLA_HEADS, axis=1)
        km_ref[rows, :] = rotm(kfull).astype(BF16)
        vm_ref[rows, :] = _dot(ckv_b, wuv_ref[...]).astype(BF16)
        gq = z[:, _O_GQ:_O_GK]
        gq = gq * lax.rsqrt(_group_mean_sq(gq, bd) + EPS) * gqn_ref[...]
        gq_ref[rows, :] = (rot64(gq) * (LOG2E * GQA_DIM ** -0.5)).astype(BF16)
        gk = z[:, _O_GK:_O_GV]
        gk = gk * lax.rsqrt(_group_mean_sq(gk, bd[:LANES, :LANES]) + EPS) * gkn_ref[...]
        gk_ref[rows, :] = rot64(gk).astype(BF16)
        gv = z[:, _O_GV:_O_DQ]
        gv_ref[rows, :] = gv.astype(BF16)
        dq_ref[rows, :] = (rot64(z[:, _O_DQ:_O_DK]) * (LOG2E * DIFF_DIM ** -0.5)).astype(BF16)
        dk = z[:, _O_DK:_O_DV]
        dk_ref[rows, :] = rot64(dk).astype(BF16)
        dv = z[:, _O_DV:_O_KPE]
        dv_ref[rows, :] = dv.astype(BF16)
        if own:
            T = ockv_ref.shape[1]
            assert sub % T == 0
            for p in range(sub // T):
                b = st * (sub // T) + p
                rl = slice(p * T, (p + 1) * T)
                ockv_ref[b] = ckv[rl]
                okpe_ref[b] = kpe[rl, :MLA_ROPE]
                for g in range(GQA_KV_HEADS):
                    ogk_ref[b, :, g, :] = gk[rl, g * GQA_DIM:(g + 1) * GQA_DIM]
                    ogv_ref[b, :, g, :] = gv[rl, g * GQA_DIM:(g + 1) * GQA_DIM]
                for hd in range(DIFF_HEADS):
                    odk_ref[b, :, hd, :] = dk[rl, hd * LANES:(hd + 1) * LANES]
                    odv_ref[b, :, hd, :] = dv[rl, hd * LANES:(hd + 1) * LANES]


def _layer_spec(a, l):
    zeros = (0,) * (a.ndim - 1)
    return pl.BlockSpec((None,) + a.shape[1:], lambda *_: (l,) + zeros)


def _mod_spec(mod_all, msel, l, tpb, tile_of=lambda i: i):
    row0, per_batch = msel
    D = mod_all.shape[-1]
    if per_batch:
        return pl.BlockSpec((None, 1, 6, D), lambda i, *_: (l, row0 + tile_of(i) // tpb, 0, 0))
    return pl.BlockSpec((None, 1, 6, D), lambda i, *_: (l, row0, 0, 0))


def _row_tile(n, T, msel, tile=ROW_TILE):
    return min(tile, T) if msel[1] else min(tile, n)


def _in_proj(x, mod_all, msel, W, l, tabs, T, own):
    n, D = x.shape
    tm = _row_tile(n, T, msel)
    rope = tabs is not None
    tpb = max(T // tm, 1)
    row = lambda i: (i, 0)
    stacked = [W[k] for k in ("norm1", "w1", "mla_q_norm", "mla_kv_norm", "gqa_q_norm", "gqa_k_norm",
                              "wuq", "wuk", "wuv")]
    ins = [x, mod_all] + stacked + [W["bd"]]
    in_specs = ([pl.BlockSpec((tm, D), row), _mod_spec(mod_all, msel, l, tpb)]
                + [_layer_spec(a, l) for a in stacked] + [pl.BlockSpec(W["bd"].shape, lambda i: (0, 0))])
    if rope:
        ins += list(tabs)
        in_specs += [pl.BlockSpec((tm, LANES), lambda i: (i % tpb, 0))] * 4
    widths = [(1024, BF16), (1024, BF16), (512, BF16), (512, BF16), (128, BF16), (128, BF16),
              (512, BF16), (512, BF16), (512, BF16)]
    out_shape = [jax.ShapeDtypeStruct((n, w), dt) for w, dt in widths]
    out_specs = [pl.BlockSpec((tm, w), row) for w, _ in widths]
    aliases = {}
    if own is not None:
        L = mod_all.shape[0]
        bpt = tm // T
        assert bpt * T == tm
        tails = [(MLA_KV_RANK,), (MLA_ROPE,), (GQA_KV_HEADS, GQA_DIM), (GQA_KV_HEADS, GQA_DIM),
                 (DIFF_HEADS, 2 * DIFF_DIM), (DIFF_HEADS, DIFF_V)]
        for k, tail in enumerate(tails):
            zeros = (0,) * (1 + len(tail))
            out_shape.append(jax.ShapeDtypeStruct((n // T, L, T) + tail, F32))
            out_specs.append(pl.BlockSpec((bpt, None, T) + tail, lambda i, zeros=zeros: (i, l) + zeros))
            if own:
                aliases[len(ins)] = len(widths) + k
                ins.append(own[k])
                in_specs.append(pl.BlockSpec(memory_space=pl.ANY))
    return pl.pallas_call(
        functools.partial(_in_body, rope, own is not None, len(own) if own else 0),
        out_shape=out_shape, grid=(n // tm,), in_specs=in_specs, out_specs=out_specs,
        input_output_aliases=aliases,
        compiler_params=_cparams(("parallel",)), name="in_proj",
    )(*ins)


def _cache_body(ckv_ref, kpe_ref, wuk_ref, wuv_ref, km_ref, vm_ref):
    ckv_b = ckv_ref[...].astype(BF16)
    kpe_at = pltpu.roll(kpe_ref[...], MLA_NOPE, 1)
    km_ref[...] = (_dot(ckv_b, wuk_ref[...]) + jnp.concatenate([kpe_at] * MLA_HEADS, axis=1)).astype(BF16)
    vm_ref[...] = _dot(ckv_b, wuv_ref[...]).astype(BF16)


def _cache_proj(ckv, kpe_pad, W, l):
    n = ckv.shape[0]
    tm = min(ROW_TILE, n)
    row = lambda i: (i, 0)
    return pl.pallas_call(
        _cache_body,
        out_shape=[jax.ShapeDtypeStruct((n, 1024), BF16), jax.ShapeDtypeStruct((n, 512), BF16)],
        grid=(n // tm,),
        in_specs=[pl.BlockSpec((tm, MLA_KV_RANK), row), pl.BlockSpec((tm, LANES), row),
                  _layer_spec(W["wuk"], l), _layer_spec(W["wuv"], l)],
        out_specs=[pl.BlockSpec((tm, 1024), row), pl.BlockSpec((tm, 512), row)],
        compiler_params=_cparams(("parallel",)), name="cache_proj",
    )(ckv, kpe_pad, W["wuk"], W["wuv"])


def _swap_halves(a):
    return pltpu.roll(a.astype(F32), LANES // 2, 1).astype(a.dtype)


def _softmax_av(q, ks, vs, v_has_ones):
    ss = [_dot_nt(q, k) for k in ks]
    mx = ss[0].max(axis=-1, keepdims=True)
    for s in ss[1:]:
        mx = jnp.maximum(mx, s.max(axis=-1, keepdims=True))
    acc, den = None, None
    for s, v in zip(ss, vs):
        p = jnp.exp2(s - mx)
        if not v_has_ones:
            d = p.sum(axis=-1, keepdims=True)
            den = d if den is None else den + d
        a = _dot(p.astype(BF16), v)
        acc = a if acc is None else acc + a
    return acc, den


def _attn_body(mode, nparts, lam_init, *refs):
    q_ref = refs[0]
    kv = [(refs[1 + 2 * i], refs[2 + 2 * i]) for i in range(nparts)]
    k = 1 + 2 * nparts
    if mode == "diff":
        lq1, lk1, lq2, lk2, sub_ref = refs[k:k + 5]
        k += 5
    o_ref = refs[k]

    lo = _lane_iota((1, LANES)) < (LANES // 2)
    hi = jnp.logical_not(lo)
    ones = jnp.ones((1, LANES), BF16)
    cols = lambda j: slice(j * LANES, (j + 1) * LANES)

    def normalise(full):
        return full / pltpu.roll(full, LANES // 2, 1)

    if mode == "diff":
        lam = (jnp.exp(jnp.sum(lq1[...] * lk1[...], axis=-1, keepdims=True))
               - jnp.exp(jnp.sum(lq2[...] * lk2[...], axis=-1, keepdims=True)) + lam_init)
    for bi in range(q_ref.shape[0]):
        if mode == "gqa":
            ks = [kr[bi].astype(BF16) for kr, _ in kv]
            vs = [vr[bi].astype(BF16) for _, vr in kv]
            ks_sw = [_swap_halves(a) for a in ks]
            vs_sw = [_swap_halves(a) for a in vs]
            for pair in range(GQA_HEADS // 2):
                qp = q_ref[bi, :, cols(pair)]
                res = []
                for L in range(2):
                    g = (2 * pair + L) // (GQA_HEADS // GQA_KV_HEADS)
                    mine = lo if L == 0 else hi
                    kk = ks if g == L else ks_sw
                    vv = [jnp.where(mine, a, ones) for a in (vs if g == L else vs_sw)]
                    full, _ = _softmax_av(jnp.where(mine, qp, jnp.zeros_like(qp)), kk, vv, True)
                    res.append(normalise(full))
                o_ref[bi, :, cols(pair)] = jnp.where(lo, res[0], res[1]).astype(o_ref.dtype)
        elif mode == "mla":
            for pair in range(MLA_HEADS // 2):
                vs = [vr[bi, :, cols(pair)] for _, vr in kv]
                res = []
                for L in range(2):
                    mine = lo if L == 0 else hi
                    kk = [kr[bi, :, cols(2 * pair + L)] for kr, _ in kv]
                    vv = [jnp.where(mine, a, ones) for a in vs]
                    full, _ = _softmax_av(q_ref[bi, :, cols(2 * pair + L)], kk, vv, True)
                    res.append(normalise(full))
                o_ref[bi, :, cols(pair)] = jnp.where(lo, res[0], res[1]).astype(o_ref.dtype)
        else:
            for h in range(DIFF_HEADS):
                qh = q_ref[bi, :, cols(h)]
                head = lambda r: (r[bi, :, h, :] if len(r.shape) == 4 else r[bi, :, cols(h)]).astype(BF16)
                kk = [head(kr) for kr, _ in kv]
                vv = [head(vr) for _, vr in kv]
                zero = jnp.zeros_like(qh)
                a1, d1 = _softmax_av(jnp.where(lo, qh, zero), kk, vv, False)
                a2, d2 = _softmax_av(jnp.where(lo, zero, qh), kk, vv, False)
                o = _rms(a1 / d1 - lam * (a2 / d2), sub_ref[...]) * (1.0 - lam_init)
                o_ref[bi, :, cols(h)] = o.astype(o_ref.dtype)


def _attention(mode, q, parts, extra, l=0, lam_init=0.0):
    B, T, C = q.shape
    tq = min(Q_TILE, T)
    per_step = {"diff": ATTN_ROWS_PER_STEP // tq, "gqa": 2, "mla": 1}[mode]
    bb = max(1, min(B, per_step)) if (T == tq and T < Q_TILE) else 1
    while B % bb:
        bb -= 1
    ins = [q]
    in_specs = [pl.BlockSpec((bb, tq, C), lambda b, i: (b, i, 0))]
    def part_spec(a):
        if a.ndim == 5:
            return pl.BlockSpec((bb, None) + a.shape[2:], lambda b, i: (b, l, 0, 0, 0))
        return pl.BlockSpec((bb,) + a.shape[1:], lambda b, i: (b, 0, 0))

    for k, v in parts:
        ins += [k, v]
        in_specs += [part_spec(k), part_spec(v)]
    for e in extra:
        ins.append(e)
        in_specs.append(_layer_spec(e, l))
    return pl.pallas_call(
        functools.partial(_attn_body, mode, len(parts), lam_init),
        out_shape=jax.ShapeDtypeStruct((B, T, 512), BF16),
        grid=(B // bb, T // tq),
        in_specs=in_specs,
        out_specs=pl.BlockSpec((bb, tq, 512), lambda b, i: (b, i, 0)),
        compiler_params=_cparams(("parallel", "parallel")), name="attn_" + mode,
    )(*ins)


def _out_body(x_ref, mod_ref, n1_ref, n2_ref, om_ref, og_ref, od_ref, wg_ref, bm_ref, bg_ref, bdf_ref,
              wo_ref, rh_ref, rl_ref, x1_ref, h2_ref, aff_ref):
    shift1, scale1, gate1 = mod_ref[0, 0:1, :], mod_ref[0, 1:2, :], mod_ref[0, 2:3, :]
    shift2, scale2 = mod_ref[0, 3:4, :], mod_ref[0, 4:5, :]
    D = x_ref.shape[1]
    sig = lambda u: 1.0 / (1.0 + jnp.exp(-u))
    valid = _lane_iota((1, LANES)) < N_EXPERTS
    sub = min(OUT_SUB_TILE, x_ref.shape[0])
    for s in range(x_ref.shape[0] // sub):
        rows = pl.ds(s * sub, sub)
        x = x_ref[rows, :]
        h = (_rms(x, n1_ref[...]) * (1.0 + scale1) + shift1).astype(BF16)
        merged = None
        for b, (o_ref, w_ref) in enumerate(((om_ref, bm_ref), (og_ref, bg_ref), (od_ref, bdf_ref))):
            t = sig(_dot(h, wg_ref[:, b * D:(b + 1) * D])) * _dot(o_ref[rows, :], w_ref[...])
            merged = t if merged is None else merged + t
        y = _dot(merged.astype(BF16), wo_ref[...])
        x1 = x + gate1 * y
        x1_ref[rows, :] = x1
        h2 = _rms(x1, n2_ref[...]) * (1.0 + scale2) + shift2
        h2_ref[rows, :] = h2
        hi = h2.astype(BF16)
        lo = (h2 - hi.astype(F32)).astype(BF16)
        logits = _dot(hi, rh_ref[...]) + _dot(lo, rh_ref[...]) + _dot(hi, rl_ref[...])
        logits = jnp.where(valid, logits, -1e30)
        e = jnp.exp(logits - logits.max(axis=-1, keepdims=True))
        aff = e / e.sum(axis=-1, keepdims=True)
        aff_ref[:, s * sub:(s + 1) * sub] = aff.T[:N_EXPERTS, :]


def _out_proj(x, mod_all, msel, om, og, od, W, l, T):
    n, D = x.shape
    tm = _row_tile(n, T, msel, OUT_ROW_TILE)
    tpb = max(T // tm, 1)
    row = lambda i: (i, 0)
    ws = [W[k] for k in ("wgate", "wbr_mla", "wbr_gqa", "wbr_diff", "w_out", "wr_hi", "wr_lo")]
    return pl.pallas_call(
        _out_body,
        out_shape=[jax.ShapeDtypeStruct((n, D), F32), jax.ShapeDtypeStruct((n, D), F32),
                   jax.ShapeDtypeStruct((N_EXPERTS, n), F32)],
        grid=(n // tm,),
        in_specs=[pl.BlockSpec((tm, D), row), _mod_spec(mod_all, msel, l, tpb), _layer_spec(W["norm1"], l),
                  _layer_spec(W["norm2"], l), pl.BlockSpec((tm, 512), row), pl.BlockSpec((tm, 512), row),
                  pl.BlockSpec((tm, 512), row)] + [_layer_spec(w, l) for w in ws],
        out_specs=[pl.BlockSpec((tm, D), row), pl.BlockSpec((tm, D), row),
                   pl.BlockSpec((N_EXPERTS, tm), lambda i: (0, i))],
        compiler_params=_cparams(("parallel",)), name="out_proj",
    )(x, mod_all, W["norm1"], W["norm2"], om, og, od, *ws)


def _route_body(cap, aff_ref, idx_ref, gate_ref, blk_ref, rot_ref):
    v = aff_ref[...]
    E, n = v.shape
    nbits = max(1, (n - 1).bit_length())
    lane = _lane_iota((E, n))

    def count(mask):
        return jnp.sum(mask.astype(I32), axis=1, keepdims=True)

    def step(i, t):
        cand = t | lax.shift_left(jnp.int32(1), 30 - i)
        return jnp.where(count(v >= lax.bitcast_convert_type(cand, F32)) >= cap, cand, t)

    thr = lax.bitcast_convert_type(lax.fori_loop(0, 31, step, jnp.zeros((E, 1), I32)), F32)

    def prefix(u):
        for b in range(nbits):
            s = 1 << b
            u = u + jnp.where(lane >= s, pltpu.roll(u, s, 1), 0)
        return u

    gt = v > thr
    eq = v == thr
    need = cap - count(gt)
    eq_i = eq.astype(I32)
    eq_rank = prefix(eq_i) - eq_i
    sel = jnp.where(gt | (eq & (eq_rank < need)), 1, 0).astype(I32)
    dist = lane + 1 - prefix(sel)
    tok = lane
    for b in range(nbits):
        s = 1 << b
        mv = sel * ((dist >> b) & 1)
        stay = sel - mv
        take = pltpu.roll(mv, n - s, 1) == 1
        tok = jnp.where(take, pltpu.roll(tok, n - s, 1), tok)
        v = jnp.where(take, pltpu.roll(v, n - s, 1), v)
        dist = jnp.where(take, pltpu.roll(dist, n - s, 1), dist)
        sel = jnp.where(take, 1, stay)
    tok = tok[:, :cap]
    idx_ref[...] = tok
    gate_ref[...] = v[:, :cap]
    blk_ref[...] = tok >> 3
    rot_ref[...] = (_lane_iota((E, cap)) - tok) & (SUBLANES - 1)


def _route(aff_t, cap):
    E, n = aff_t.shape
    return pl.pallas_call(
        functools.partial(_route_body, cap),
        out_shape=[jax.ShapeDtypeStruct((E, cap), I32), jax.ShapeDtypeStruct((E, cap), F32),
                   jax.ShapeDtypeStruct((E, cap), I32), jax.ShapeDtypeStruct((E, cap), I32)],
        compiler_params=pltpu.CompilerParams(vmem_limit_bytes=VMEM_LIMIT), name="route",
    )(aff_t)


def _gather_body(tc, blk_ref, rot_ref, xs_ref, o_ref):
    i = pl.program_id(0)
    base = i * tc

    group = 16
    D = xs_ref.shape[2]
    sub = lax.broadcasted_iota(I32, (SUBLANES, D), 0)

    def body(g, _):
        halves = []
        for hf in range(group // SUBLANES):
            acc = jnp.zeros((SUBLANES, D), F32)
            for j in range(SUBLANES):
                s = base + g * group + hf * SUBLANES + j
                blk8 = pltpu.roll(xs_ref[blk_ref[s]], rot_ref[s], 0)
                acc = jnp.where(sub == j, blk8, acc)
            halves.append(acc)
        rows = pl.ds(pl.multiple_of(g * group, group), group)
        o_ref[rows, :] = jnp.concatenate(halves, axis=0).astype(BF16)
        return 0

    lax.fori_loop(0, tc // group, body, 0)


def _gather(blk, rot, xs, tc):
    E, cap = blk.shape
    n, D = xs.shape
    tiles = cap // tc
    return pl.pallas_call(
        functools.partial(_gather_body, tc),
        out_shape=jax.ShapeDtypeStruct((E * cap, D), BF16),
        grid_spec=pltpu.PrefetchScalarGridSpec(
            num_scalar_prefetch=2, grid=(E * tiles,),
            in_specs=[pl.BlockSpec((n // SUBLANES, SUBLANES, D), lambda i, b, r: (0, 0, 0),
                                   pipeline_mode=pl.Buffered(1))],
            out_specs=pl.BlockSpec((tc, D), lambda i, b, r: (i, 0))),
        compiler_params=_cparams(("arbitrary",)), name="moe_gather",
    )(blk.reshape(-1), rot.reshape(-1), xs.reshape(n // SUBLANES, SUBLANES, D))


def _ffn_body(xe_ref, wg_ref, wu_ref, wd_ref, o_ref):
    xe = xe_ref[...]
    hg = _dot(xe, wg_ref[0, 0].astype(BF16))
    hu = _dot(xe, wu_ref[0, 0].astype(BF16))
    a = (hg / (1.0 + jnp.exp(-hg))) * hu
    o_ref[...] = _dot(a.astype(BF16), wd_ref[0, 0].astype(BF16))


def _ffn(xe, wg, wu, wd, l, tc):
    _, E, D, FF = wg.shape
    rows = xe.shape[0]
    tiles = rows // E // tc
    wmap = lambda e, t: (l, e, 0, 0)
    row = lambda e, t: (e * tiles + t, 0)
    return pl.pallas_call(
        _ffn_body,
        out_shape=jax.ShapeDtypeStruct((rows, D), F32),
        grid=(E, tiles),
        in_specs=[pl.BlockSpec((tc, D), row), pl.BlockSpec((1, 1, D, FF), wmap),
                  pl.BlockSpec((1, 1, D, FF), wmap), pl.BlockSpec((1, 1, FF, D), wmap)],
        out_specs=pl.BlockSpec((tc, D), row),
        compiler_params=_cparams(("parallel", "parallel")), name="moe_ffn",
    )(xe, wg, wu, wd)


def _combine_body(tc, tiles_per_e, n_scatter, final, idx_ref, gate_ref, ye_ref, x1_ref, mod_ref, fn_ref,
                  o_ref, acc_ref):
    i = pl.program_id(0)

    @pl.when(i == 0)
    def _():
        acc_ref[...] = jnp.zeros_like(acc_ref)

    @pl.when(i < n_scatter)
    def _():
        base = i * tc

        sub = lax.broadcasted_iota(I32, (SUBLANES, LANES), 0)

        nlane = ye_ref.shape[1] // LANES
        ways = 4

        def body(gq, _):
            s0 = pl.multiple_of(gq * (ways * SUBLANES), ways * SUBLANES)
            ye8 = [ye_ref[pl.ds(s0 + w * SUBLANES, SUBLANES), :] for w in range(ways)]
            for j in range(SUBLANES):
                upd = []
                for w in range(ways):
                    r = idx_ref[base + s0 + w * SUBLANES + j]
                    g = gate_ref[base + s0 + w * SUBLANES + j]
                    r8 = pl.multiple_of((r >> 3) << 3, SUBLANES)
                    rolled = pltpu.roll(ye8[w], (r - j) & (SUBLANES - 1), 0)
                    gmask = jnp.where(sub == (r & (SUBLANES - 1)), g, 0.0)
                    upd.append((r8, acc_ref[pl.ds(r8, SUBLANES), :]
                                + rolled * jnp.concatenate([gmask] * nlane, axis=1)))
                for r8, val in upd:
                    acc_ref[pl.ds(r8, SUBLANES), :] = val
            return 0

        lax.fori_loop(0, tc // (ways * SUBLANES), body, 0)

    @pl.when(i >= n_scatter)
    def _():
        tm = x1_ref.shape[0]
        j = i - n_scatter
        rows = pl.ds(pl.multiple_of(j * tm, tm), tm)
        xn = x1_ref[...] + mod_ref[0, 5:6, :] * acc_ref[rows, :]
        o_ref[...] = _rms(xn, fn_ref[...]) if final else xn


def _combine(idx, gate, ye, x1, mod_all, msel, l, final_norm, tc, T, final):
    E, cap = idx.shape
    n, D = x1.shape
    tm = _row_tile(n, T, msel)
    tiles = cap // tc
    n_sc = E * tiles
    tpb = max(T // tm, 1)
    out_tile = lambda i: jnp.maximum(i - n_sc, 0)
    out_row = lambda i, a, b: (out_tile(i), 0)
    return pl.pallas_call(
        functools.partial(_combine_body, tc, tiles, n_sc, final),
        out_shape=jax.ShapeDtypeStruct((n, D), F32),
        grid_spec=pltpu.PrefetchScalarGridSpec(
            num_scalar_prefetch=2, grid=(n_sc + n // tm,),
            in_specs=[pl.BlockSpec((tc, D), lambda i, a, b: (jnp.minimum(i, n_sc - 1), 0)),
                      pl.BlockSpec((tm, D), out_row), _mod_spec(mod_all, msel, l, tpb, out_tile),
                      pl.BlockSpec((1, D), lambda i, a, b: (0, 0))],
            out_specs=pl.BlockSpec((tm, D), out_row),
            scratch_shapes=[pltpu.VMEM((n, D), F32)]),
        compiler_params=_cparams(("arbitrary",)), name="moe_combine",
    )(idx.reshape(-1), gate.reshape(-1), ye, x1, mod_all, final_norm)


def _rope_tables(T):
    t = jnp.arange(T)
    row = (t // GRID_W).astype(F32)
    col = (t % GRID_W).astype(F32)

    def table(head_w, lane_off, rep):
        half = head_w // 4
        freqs = ROPE_BASE ** (-jnp.arange(half, dtype=F32) / half)
        blk = []
        for pos in (row, col):
            ang = pos[:, None] * freqs[None, :]
            blk.append((jnp.cos(ang), jnp.sin(ang)))
        cos = jnp.concatenate([blk[0][0], blk[0][0], blk[1][0], blk[1][0]], axis=1)
        sin = jnp.concatenate([-blk[0][1], blk[0][1], -blk[1][1], blk[1][1]], axis=1)
        cos = jnp.concatenate([cos] * rep, axis=1)
        sin = jnp.concatenate([sin] * rep, axis=1)
        pad_r = LANES - lane_off - cos.shape[1]
        cos = jnp.pad(cos, ((0, 0), (lane_off, pad_r)), constant_values=1.0)
        sin = jnp.pad(sin, ((0, 0), (lane_off, pad_r)))
        return cos, sin

    c64, s64 = table(GQA_DIM, 0, 2)
    cm, sm = table(MLA_ROPE, MLA_NOPE, 1)
    return c64, s64, cm, sm


def _prep_weights(w_in, norm1, norm2, mla_q_norm, mla_kv_norm, mla_w_uq, mla_w_uk, mla_w_uv,
                  gqa_q_norm, gqa_k_norm, w_br_mla, w_br_gqa, w_br_diff, w_out, w_router):
    L, D = w_in.shape[0], D_MODEL
    w1 = jnp.concatenate([w_in[:, :, :640], w_in[:, :, 672:2976], w_in[:, :, 640:672],
                          jnp.zeros((L, D, _W1_COLS - _O_KPE - MLA_ROPE), F32)], axis=2).astype(BF16)
    wgate = w_in[:, :, 2976:].astype(BF16)
    uq = mla_w_uq.reshape(L, MLA_Q_RANK, MLA_HEADS, MLA_NOPE + MLA_ROPE)
    wuq = jnp.pad(uq, ((0, 0), (0, 0), (0, 0), (0, HEAD_PAD - MLA_NOPE - MLA_ROPE))).reshape(L, MLA_Q_RANK, -1)
    wuk = jnp.pad(mla_w_uk, ((0, 0), (0, 0), (0, 0), (0, HEAD_PAD - MLA_NOPE))).reshape(L, MLA_KV_RANK, -1)
    wuv = mla_w_uv.reshape(L, MLA_KV_RANK, MLA_HEADS * MLA_V)
    grp = jnp.arange(512) // GQA_DIM
    bd = (grp[:, None] == grp[None, :]).astype(BF16)
    wr = jnp.pad(w_router, ((0, 0), (0, 0), (0, LANES - N_EXPERTS)))
    wr_hi = wr.astype(BF16)
    wr_lo = (wr - wr_hi.astype(F32)).astype(BF16)
    vec = lambda a: a.reshape(L, 1, -1)
    return {
        "norm1": vec(norm1), "norm2": vec(norm2), "w1": w1,
        "mla_q_norm": vec(mla_q_norm), "mla_kv_norm": vec(mla_kv_norm),
        "gqa_q_norm": vec(jnp.tile(gqa_q_norm, (1, GQA_HEADS))),
        "gqa_k_norm": vec(jnp.tile(gqa_k_norm, (1, GQA_KV_HEADS))),
        "wuq": wuq.astype(BF16), "wuk": wuk.astype(BF16), "wuv": wuv.astype(BF16), "bd": bd,
        "wgate": wgate, "wbr_mla": w_br_mla.astype(BF16),
        "wbr_gqa": w_br_gqa.astype(BF16), "wbr_diff": w_br_diff.astype(BF16),
        "w_out": w_out.astype(BF16), "wr_hi": wr_hi, "wr_lo": wr_lo,
    }


def _trunk(x, mod_all, msel, W, l, ew, lam, lam_init, tabs, ctx, B, T, final_norm, final, own):
    n = B * T
    outs = _in_proj(x, mod_all, msel, W, l, tabs, T, own)
    qm, km, vm, gq, gk, gv, dq, dk, dv = [a.reshape(B, T, -1) for a in outs[:9]]
    if ctx is None:
        pm, pg, pd = [(km, vm)], [(gk, gv)], [(dk, dv)]
    else:
        ckm, cvm, cgk, cgv, cdk, cdv = ctx
        pm, pg, pd = [(ckm, cvm), (km, vm)], [(cgk, cgv), (gk, gv)], [(cdk, cdv), (dk, dv)]
    om = _attention("mla", qm, pm, []).reshape(n, -1)
    og = _attention("gqa", gq, pg, []).reshape(n, -1)
    od = _attention("diff", dq, pd, list(lam), l, lam_init).reshape(n, -1)
    x1, h2, aff = _out_proj(x, mod_all, msel, om, og, od, W, l, T)
    cap = CAPACITY_FACTOR * n // N_EXPERTS
    idx, gate, blk, rot = _route(aff, cap)
    tc = min(MOE_ROW_TILE, cap)
    xe = _gather(blk, rot, h2, tc)
    ye = _ffn(xe, ew[0], ew[1], ew[2], ew[3], min(FFN_TILE, cap))
    xn = _combine(idx, gate, ye, x1, mod_all, msel, l, final_norm, tc, T, final)
    return xn, outs[9:]


def kernel(x_prompt, x_sample, cache_mla_ckv, cache_mla_kpe, cache_gqa_k, cache_gqa_v, cache_diff_k, cache_diff_v, c, c_ctx, w_ada, b_ada, norm1, norm2, w_in, mla_q_norm, mla_kv_norm, mla_w_uq, mla_w_uk, mla_w_uv, gqa_q_norm, gqa_k_norm, diff_lam_q1, diff_lam_k1, diff_lam_q2, diff_lam_k2, diff_subln, w_br_mla, w_br_gqa, w_br_diff, w_out, w_router, w_gate, w_up, w_down, final_norm):
    Bp, Tp, D = x_prompt.shape
    Bs, Ts, _ = x_sample.shape
    L = w_in.shape[0]
    P = cache_mla_ckv.shape[2]
    assert D == D_MODEL and Ts % GRID_W == 0

    nrow = -(-(1 + Bs) // 8) * 8
    cond = jnp.zeros((nrow, D), F32).at[0].set(c_ctx).at[1:1 + Bs].set(c)
    mod_all = _ada(cond, w_ada, b_ada).reshape(L, nrow, 6, D)

    tabs = _rope_tables(Ts)
    fn = final_norm[None]
    xp = x_prompt.reshape(Bp * Tp, D)
    xs = x_sample.reshape(Bs * Ts, D)
    own = ()
    W = _prep_weights(w_in, norm1, norm2, mla_q_norm, mla_kv_norm, mla_w_uq, mla_w_uk, mla_w_uv,
                      gqa_q_norm, gqa_k_norm, w_br_mla, w_br_gqa, w_br_diff, w_out, w_router)
    lam = tuple(a.reshape(L, 1, -1) for a in (diff_lam_q1, diff_lam_k1, diff_lam_q2, diff_lam_k2, diff_subln))
    kpe_pad_all = jnp.pad(cache_mla_kpe, ((0, 0), (0, 0), (0, 0), (0, LANES - MLA_ROPE)))
    for l in range(L):
        ew = (w_gate, w_up, w_down, l)
        lam_init = 0.8 - 0.6 * math.exp(-0.3 * l)
        final = l == L - 1
        xp, own = _trunk(xp, mod_all, (0, False), W, l, ew, lam, lam_init, None, None, Bp, Tp, fn, final, own)
        kpe_pad = kpe_pad_all[:, l].reshape(Bs * P, LANES)
        ckm, cvm = _cache_proj(cache_mla_ckv[:, l].reshape(Bs * P, MLA_KV_RANK), kpe_pad, W, l)
        ctx = (ckm.reshape(Bs, P, -1), cvm.reshape(Bs, P, -1),
               cache_gqa_k[:, l].reshape(Bs, P, -1), cache_gqa_v[:, l].reshape(Bs, P, -1),
               cache_diff_k, cache_diff_v)
        xs, _ = _trunk(xs, mod_all, (1, True), W, l, ew, lam, lam_init, tabs, ctx, Bs, Ts, fn, final, None)

    return (xp.reshape(Bp, Tp, D), xs.reshape(Bs, Ts, D)) + tuple(own)
```

```python
import functools
import math

import jax
import jax.numpy as jnp
from jax import lax
from jax.experimental import pallas as pl
from jax.experimental.pallas import tpu as pltpu

F32 = jnp.float32
BF16 = jnp.bfloat16
I32 = jnp.int32

D_MODEL = 1024
GRID_W = 64
ROPE_BASE = 10000.0
EPS = 1e-6
MLA_HEADS, MLA_NOPE, MLA_ROPE, MLA_V = 8, 64, 32, 64
MLA_Q_RANK, MLA_KV_RANK = 384, 256
GQA_HEADS, GQA_KV_HEADS, GQA_DIM = 8, 2, 64
DIFF_HEADS, DIFF_DIM = 4, 64
DIFF_V = 2 * DIFF_DIM
N_EXPERTS, EXPERT_FF, CAPACITY_FACTOR = 16, 1024, 2

LANES = 128
SUBLANES = 8
HEAD_PAD = 128
ROW_TILE = 512
IN_SUB_TILE = 256
OUT_ROW_TILE = 512
OUT_SUB_TILE = 256
Q_TILE = 1024
ATTN_ROWS_PER_STEP = 1024
FFN_TILE = 1024
MOE_ROW_TILE = 512
VMEM_LIMIT = 56 * 1024 * 1024

W_QK_MLA = MLA_HEADS * HEAD_PAD
W_V_MLA = MLA_HEADS * MLA_V
W_Q_GQA, W_KV_GQA = GQA_HEADS * GQA_DIM, GQA_KV_HEADS * GQA_DIM
W_DIFF = DIFF_HEADS * 2 * DIFF_DIM
W_ATTN_OUT = MLA_HEADS * MLA_V
assert W_ATTN_OUT == GQA_HEADS * GQA_DIM == DIFF_HEADS * DIFF_V

_IN_SIZES = (MLA_Q_RANK, MLA_KV_RANK, MLA_ROPE, W_Q_GQA, W_KV_GQA, W_KV_GQA, W_DIFF, W_DIFF, W_DIFF)
_IN_CQ, _IN_CKV, _IN_KPE, _IN_GQ = (sum(_IN_SIZES[:k]) for k in range(4))
_IN_GATES = sum(_IN_SIZES)
_O_CQ = 0
_O_CKV = _O_CQ + MLA_Q_RANK
_O_GQ = _O_CKV + MLA_KV_RANK
_O_GK = _O_GQ + W_Q_GQA
_O_GV = _O_GK + W_KV_GQA
_O_DQ = _O_GV + W_KV_GQA
_O_DK = _O_DQ + W_DIFF
_O_DV = _O_DK + W_DIFF
_O_KPE = _O_DV + W_DIFF
_W1_COLS = _O_KPE + LANES
assert all(o % LANES == 0 for o in (_O_CKV, _O_GQ, _O_GK, _O_GV, _O_DQ, _O_DK, _O_DV, _O_KPE))
LOG2E = 1.4426950408889634


def _cparams(sem):
    return pltpu.CompilerParams(dimension_semantics=sem, vmem_limit_bytes=VMEM_LIMIT)


def _dot(a, b):
    return jnp.dot(a, b, preferred_element_type=F32)


def _dot_nt(a, b):
    return lax.dot_general(a, b, (((1,), (1,)), ((), ())), preferred_element_type=F32)


def _rms(x, g):
    return x * lax.rsqrt(jnp.mean(x * x, axis=-1, keepdims=True) + EPS) * g


def _lane_iota(shape):
    return lax.broadcasted_iota(I32, shape, len(shape) - 1)


def _rope(x, cos, sin, width):
    outs = []
    even = ((_lane_iota((1, LANES)) // width) % 2) == 0
    for j in range(x.shape[1] // LANES):
        xb = x[:, j * LANES:(j + 1) * LANES]
        partner = jnp.where(even, pltpu.roll(xb, LANES - width, 1), pltpu.roll(xb, width, 1))
        outs.append(xb * cos + partner * sin)
    return outs[0] if len(outs) == 1 else jnp.concatenate(outs, axis=1)


def _group_mean_sq(x, bd):
    return _dot((x * x).astype(BF16), bd) * (1.0 / GQA_DIM)


def _ada_body(c_ref, w_ref, b_ref, o_ref):
    c = c_ref[...]
    s = c / (1.0 + jnp.exp(-c))
    o_ref[0] = _dot(s.astype(BF16), w_ref[0].astype(BF16)) + b_ref[0]


def _ada(cond, w_ada, b_ada):
    L, D, N = w_ada.shape
    R = cond.shape[0]
    tn = 512
    return pl.pallas_call(
        _ada_body,
        out_shape=jax.ShapeDtypeStruct((L, R, N), F32),
        grid=(L, N // tn),
        in_specs=[pl.BlockSpec((R, D), lambda l, j: (0, 0)),
                  pl.BlockSpec((1, D, tn), lambda l, j: (l, 0, j)),
                  pl.BlockSpec((1, 1, tn), lambda l, j: (l, 0, j))],
        out_specs=pl.BlockSpec((1, R, tn), lambda l, j: (l, 0, j)),
        compiler_params=_cparams(("parallel", "parallel")),
        name="ada",
    )(cond, w_ada, b_ada.reshape(L, 1, N))


def _in_body(rope, own, n_alias, *refs):
    (x_ref, mod_ref, n1_ref, w_ref, qn_ref, kvn_ref, gqn_ref, gkn_ref, wuq_ref, wuk_ref, wuv_ref,
     bd_ref) = refs[:12]
    k = 12
    if rope:
        c64_ref, s64_ref, cm_ref, sm_ref = refs[k:k + 4]
        k += 4
    k += n_alias
    (qm_ref, km_ref, vm_ref, gq_ref, gk_ref, gv_ref, dq_ref, dk_ref, dv_ref) = refs[k:k + 9]
    k += 9
    if own:
        ockv_ref, okpe_ref, ogk_ref, ogv_ref, odk_ref, odv_ref = refs[k:k + 6]

    shift = mod_ref[0, 0:1, :]
    scale = mod_ref[0, 1:2, :]
    bd = bd_ref[...]
    sub = min(IN_SUB_TILE, x_ref.shape[0])
    for st in range(x_ref.shape[0] // sub):
        rows = pl.ds(st * sub, sub)
        x = x_ref[rows, :]
        h = _rms(x, n1_ref[...]) * (1.0 + scale) + shift
        z = _dot(h.astype(BF16), w_ref[...])

        def rot64(u):
            return _rope(u, c64_ref[rows, :], s64_ref[rows, :], 16) if rope else u

        def rotm(u):
            return _rope(u, cm_ref[rows, :], sm_ref[rows, :], 8) if rope else u

        cq = _rms(z[:, _O_CQ:_O_CKV], qn_ref[...])
        q = _dot(cq.astype(BF16), wuq_ref[...])
        qm_ref[rows, :] = (rotm(q) * (LOG2E * (MLA_NOPE + MLA_ROPE) ** -0.5)).astype(BF16)
        ckv = _rms(z[:, _O_CKV:_O_GQ], kvn_ref[...])
        kpe = z[:, _O_KPE:_W1_COLS]
        ckv_b = ckv.astype(BF16)
        kpe_at = pltpu.roll(kpe, MLA_NOPE, 1)
        knope = _dot(ckv_b, wuk_ref[...])
        kfull = knope + jnp.concatenate([kpe_at] * MLA_HEADS, axis=1)
        km_ref[rows, :] = rotm(kfull).astype(BF16)
        vm_ref[rows, :] = _dot(ckv_b, wuv_ref[...]).astype(BF16)
        gq = z[:, _O_GQ:_O_GK]
        gq = gq * lax.rsqrt(_group_mean_sq(gq, bd) + EPS) * gqn_ref[...]
        gq_ref[rows, :] = (rot64(gq) * (LOG2E * GQA_DIM ** -0.5)).astype(BF16)
        gk = z[:, _O_GK:_O_GV]
        gk = gk * lax.rsqrt(_group_mean_sq(gk, bd[:LANES, :LANES]) + EPS) * gkn_ref[...]
        gk_ref[rows, :] = rot64(gk).astype(BF16)
        gv = z[:, _O_GV:_O_DQ]
        gv_ref[rows, :] = gv.astype(BF16)
        dq_ref[rows, :] = (rot64(z[:, _O_DQ:_O_DK]) * (LOG2E * DIFF_DIM ** -0.5)).astype(BF16)
        dk = z[:, _O_DK:_O_DV]
        dk_ref[rows, :] = rot64(dk).astype(BF16)
        dv = z[:, _O_DV:_O_KPE]
        dv_ref[rows, :] = dv.astype(BF16)
        if own:
            T = ockv_ref.shape[1]
            assert sub % T == 0
            for p in range(sub // T):
                b = st * (sub // T) + p
                rl = slice(p * T, (p + 1) * T)
                ockv_ref[b] = ckv[rl]
                okpe_ref[b] = kpe[rl, :MLA_ROPE]
                for g in range(GQA_KV_HEADS):
                    ogk_ref[b, :, g, :] = gk[rl, g * GQA_DIM:(g + 1) * GQA_DIM]
                    ogv_ref[b, :, g, :] = gv[rl, g * GQA_DIM:(g + 1) * GQA_DIM]
                for hd in range(DIFF_HEADS):
                    odk_ref[b, :, hd, :] = dk[rl, hd * LANES:(hd + 1) * LANES]
                    odv_ref[b, :, hd, :] = dv[rl, hd * LANES:(hd + 1) * LANES]


def _layer_spec(a, l):
    zeros = (0,) * (a.ndim - 1)
    return pl.BlockSpec((None,) + a.shape[1:], lambda *_: (l,) + zeros)


def _mod_spec(mod_all, msel, l, tpb, tile_of=lambda i: i):
    row0, per_batch = msel
    D = mod_all.shape[-1]
    if per_batch:
        return pl.BlockSpec((None, 1, 6, D), lambda i, *_: (l, row0 + tile_of(i) // tpb, 0, 0))
    return pl.BlockSpec((None, 1, 6, D), lambda i, *_: (l, row0, 0, 0))


def _row_tile(n, T, msel, tile=ROW_TILE):
    return min(tile, T) if msel[1] else min(tile, n)


def _in_proj(x, mod_all, msel, W, l, tabs, T, own):
    n, D = x.shape
    tm = _row_tile(n, T, msel)
    rope = tabs is not None
    tpb = max(T // tm, 1)
    row = lambda i: (i, 0)
    stacked = [W[k] for k in ("norm1", "w1", "mla_q_norm", "mla_kv_norm", "gqa_q_norm", "gqa_k_norm",
                              "wuq", "wuk", "wuv")]
    ins = [x, mod_all] + stacked + [W["bd"]]
    in_specs = ([pl.BlockSpec((tm, D), row), _mod_spec(mod_all, msel, l, tpb)]
                + [_layer_spec(a, l) for a in stacked] + [pl.BlockSpec(W["bd"].shape, lambda i: (0, 0))])
    if rope:
        ins += list(tabs)
        in_specs += [pl.BlockSpec((tm, LANES), lambda i: (i % tpb, 0))] * 4
    widths = [(W_QK_MLA, BF16), (W_QK_MLA, BF16), (W_V_MLA, BF16), (W_Q_GQA, BF16), (W_KV_GQA, BF16),
              (W_KV_GQA, BF16), (W_DIFF, BF16), (W_DIFF, BF16), (W_DIFF, BF16)]
    out_shape = [jax.ShapeDtypeStruct((n, w), dt) for w, dt in widths]
    out_specs = [pl.BlockSpec((tm, w), row) for w, _ in widths]
    aliases = {}
    if own is not None:
        L = mod_all.shape[0]
        bpt = tm // T
        assert bpt * T == tm
        tails = [(MLA_KV_RANK,), (MLA_ROPE,), (GQA_KV_HEADS, GQA_DIM), (GQA_KV_HEADS, GQA_DIM),
                 (DIFF_HEADS, 2 * DIFF_DIM), (DIFF_HEADS, DIFF_V)]
        for k, tail in enumerate(tails):
            zeros = (0,) * (1 + len(tail))
            out_shape.append(jax.ShapeDtypeStruct((n // T, L, T) + tail, F32))
            out_specs.append(pl.BlockSpec((bpt, None, T) + tail, lambda i, zeros=zeros: (i, l) + zeros))
            if own:
                aliases[len(ins)] = len(widths) + k
                ins.append(own[k])
                in_specs.append(pl.BlockSpec(memory_space=pl.ANY))
    return pl.pallas_call(
        functools.partial(_in_body, rope, own is not None, len(own) if own else 0),
        out_shape=out_shape, grid=(n // tm,), in_specs=in_specs, out_specs=out_specs,
        input_output_aliases=aliases,
        compiler_params=_cparams(("parallel",)), name="in_proj",
    )(*ins)


def _cache_body(ckv_ref, kpe_ref, wuk_ref, wuv_ref, km_ref, vm_ref):
    ckv_b = ckv_ref[...].astype(BF16)
    kpe_at = pltpu.roll(kpe_ref[...], MLA_NOPE, 1)
    km_ref[...] = (_dot(ckv_b, wuk_ref[...]) + jnp.concatenate([kpe_at] * MLA_HEADS, axis=1)).astype(BF16)
    vm_ref[...] = _dot(ckv_b, wuv_ref[...]).astype(BF16)


def _cache_proj(ckv, kpe_pad, W, l):
    n = ckv.shape[0]
    tm = min(ROW_TILE, n)
    row = lambda i: (i, 0)
    return pl.pallas_call(
        _cache_body,
        out_shape=[jax.ShapeDtypeStruct((n, W_QK_MLA), BF16), jax.ShapeDtypeStruct((n, W_V_MLA), BF16)],
        grid=(n // tm,),
        in_specs=[pl.BlockSpec((tm, MLA_KV_RANK), row), pl.BlockSpec((tm, LANES), row),
                  _layer_spec(W["wuk"], l), _layer_spec(W["wuv"], l)],
        out_specs=[pl.BlockSpec((tm, W_QK_MLA), row), pl.BlockSpec((tm, W_V_MLA), row)],
        compiler_params=_cparams(("parallel",)), name="cache_proj",
    )(ckv, kpe_pad, W["wuk"], W["wuv"])


def _swap_halves(a):
    return pltpu.roll(a.astype(F32), LANES // 2, 1).astype(a.dtype)


def _softmax_av(q, ks, vs, v_has_ones):
    ss = [_dot_nt(q, k) for k in ks]
    mx = ss[0].max(axis=-1, keepdims=True)
    for s in ss[1:]:
        mx = jnp.maximum(mx, s.max(axis=-1, keepdims=True))
    acc, den = None, None
    for s, v in zip(ss, vs):
        p = jnp.exp2(s - mx)
        if not v_has_ones:
            d = p.sum(axis=-1, keepdims=True)
            den = d if den is None else den + d
        a = _dot(p.astype(BF16), v)
        acc = a if acc is None else acc + a
    return acc, den


def _attn_body(mode, nparts, lam_init, *refs):
    q_ref = refs[0]
    kv = [(refs[1 + 2 * i], refs[2 + 2 * i]) for i in range(nparts)]
    k = 1 + 2 * nparts
    if mode == "diff":
        lq1, lk1, lq2, lk2, sub_ref = refs[k:k + 5]
        k += 5
    o_ref = refs[k]

    lo = _lane_iota((1, LANES)) < (LANES // 2)
    hi = jnp.logical_not(lo)
    ones = jnp.ones((1, LANES), BF16)
    cols = lambda j: slice(j * LANES, (j + 1) * LANES)

    def normalise(full):
        return full / pltpu.roll(full, LANES // 2, 1)

    if mode == "diff":
        lam = (jnp.exp(jnp.sum(lq1[...] * lk1[...], axis=-1, keepdims=True))
               - jnp.exp(jnp.sum(lq2[...] * lk2[...], axis=-1, keepdims=True)) + lam_init)
    for bi in range(q_ref.shape[0]):
        if mode == "gqa":
            ks = [kr[bi].astype(BF16) for kr, _ in kv]
            vs = [vr[bi].astype(BF16) for _, vr in kv]
            ks_sw = [_swap_halves(a) for a in ks]
            vs_sw = [_swap_halves(a) for a in vs]
            for pair in range(GQA_HEADS // 2):
                qp = q_ref[bi, :, cols(pair)]
                res = []
                for L in range(2):
                    g = (2 * pair + L) // (GQA_HEADS // GQA_KV_HEADS)
                    mine = lo if L == 0 else hi
                    kk = ks if g == L else ks_sw
                    vv = [jnp.where(mine, a, ones) for a in (vs if g == L else vs_sw)]
                    full, _ = _softmax_av(jnp.where(mine, qp, jnp.zeros_like(qp)), kk, vv, True)
                    res.append(normalise(full))
                o_ref[bi, :, cols(pair)] = jnp.where(lo, res[0], res[1]).astype(o_ref.dtype)
        elif mode == "mla":
            for pair in range(MLA_HEADS // 2):
                vs = [vr[bi, :, cols(pair)] for _, vr in kv]
                res = []
                for L in range(2):
                    mine = lo if L == 0 else hi
                    kk = [kr[bi, :, cols(2 * pair + L)] for kr, _ in kv]
                    vv = [jnp.where(mine, a, ones) for a in vs]
                    full, _ = _softmax_av(q_ref[bi, :, cols(2 * pair + L)], kk, vv, True)
                    res.append(normalise(full))
                o_ref[bi, :, cols(pair)] = jnp.where(lo, res[0], res[1]).astype(o_ref.dtype)
        else:
            for h in range(DIFF_HEADS):
                qh = q_ref[bi, :, cols(h)]
                head = lambda r: (r[bi, :, h, :] if len(r.shape) == 4 else r[bi, :, cols(h)]).astype(BF16)
                kk = [head(kr) for kr, _ in kv]
                vv = [head(vr) for _, vr in kv]
                zero = jnp.zeros_like(qh)
                a1, d1 = _softmax_av(jnp.where(lo, qh, zero), kk, vv, False)
                a2, d2 = _softmax_av(jnp.where(lo, zero, qh), kk, vv, False)
                o = _rms(a1 / d1 - lam * (a2 / d2), sub_ref[...]) * (1.0 - lam_init)
                o_ref[bi, :, cols(h)] = o.astype(o_ref.dtype)


def _attention(mode, q, parts, extra, l=0, lam_init=0.0):
    B, T, C = q.shape
    tq = min(Q_TILE, T)
    per_step = {"diff": ATTN_ROWS_PER_STEP // tq, "gqa": 2, "mla": 1}[mode]
    bb = max(1, min(B, per_step)) if (T == tq and T < Q_TILE) else 1
    while B % bb:
        bb -= 1
    ins = [q]
    in_specs = [pl.BlockSpec((bb, tq, C), lambda b, i: (b, i, 0))]
    def part_spec(a):
        if a.ndim == 5:
            return pl.BlockSpec((bb, None) + a.shape[2:], lambda b, i: (b, l, 0, 0, 0))
        return pl.BlockSpec((bb,) + a.shape[1:], lambda b, i: (b, 0, 0))

    for k, v in parts:
        ins += [k, v]
        in_specs += [part_spec(k), part_spec(v)]
    for e in extra:
        ins.append(e)
        in_specs.append(_layer_spec(e, l))
    return pl.pallas_call(
        functools.partial(_attn_body, mode, len(parts), lam_init),
        out_shape=jax.ShapeDtypeStruct((B, T, W_ATTN_OUT), BF16),
        grid=(B // bb, T // tq),
        in_specs=in_specs,
        out_specs=pl.BlockSpec((bb, tq, W_ATTN_OUT), lambda b, i: (b, i, 0)),
        compiler_params=_cparams(("parallel", "parallel")), name="attn_" + mode,
    )(*ins)


def _out_body(x_ref, mod_ref, n1_ref, n2_ref, om_ref, og_ref, od_ref, wg_ref, bm_ref, bg_ref, bdf_ref,
              wo_ref, rh_ref, rl_ref, x1_ref, h2_ref, aff_ref):
    shift1, scale1, gate1 = mod_ref[0, 0:1, :], mod_ref[0, 1:2, :], mod_ref[0, 2:3, :]
    shift2, scale2 = mod_ref[0, 3:4, :], mod_ref[0, 4:5, :]
    D = x_ref.shape[1]
    sig = lambda u: 1.0 / (1.0 + jnp.exp(-u))
    valid = _lane_iota((1, LANES)) < N_EXPERTS
    sub = min(OUT_SUB_TILE, x_ref.shape[0])
    for s in range(x_ref.shape[0] // sub):
        rows = pl.ds(s * sub, sub)
        x = x_ref[rows, :]
        h = (_rms(x, n1_ref[...]) * (1.0 + scale1) + shift1).astype(BF16)
        merged = None
        for b, (o_ref, w_ref) in enumerate(((om_ref, bm_ref), (og_ref, bg_ref), (od_ref, bdf_ref))):
            t = sig(_dot(h, wg_ref[:, b * D:(b + 1) * D])) * _dot(o_ref[rows, :], w_ref[...])
            merged = t if merged is None else merged + t
        y = _dot(merged.astype(BF16), wo_ref[...])
        x1 = x + gate1 * y
        x1_ref[rows, :] = x1
        h2 = _rms(x1, n2_ref[...]) * (1.0 + scale2) + shift2
        h2_ref[rows, :] = h2
        hi = h2.astype(BF16)
        lo = (h2 - hi.astype(F32)).astype(BF16)
        logits = _dot(hi, rh_ref[...]) + _dot(lo, rh_ref[...]) + _dot(hi, rl_ref[...])
        logits = jnp.where(valid, logits, -1e30)
        e = jnp.exp(logits - logits.max(axis=-1, keepdims=True))
        aff = e / e.sum(axis=-1, keepdims=True)
        aff_ref[:, s * sub:(s + 1) * sub] = aff.T[:N_EXPERTS, :]


def _out_proj(x, mod_all, msel, om, og, od, W, l, T):
    n, D = x.shape
    tm = _row_tile(n, T, msel, OUT_ROW_TILE)
    tpb = max(T // tm, 1)
    row = lambda i: (i, 0)
    ws = [W[k] for k in ("wgate", "wbr_mla", "wbr_gqa", "wbr_diff", "w_out", "wr_hi", "wr_lo")]
    return pl.pallas_call(
        _out_body,
        out_shape=[jax.ShapeDtypeStruct((n, D), F32), jax.ShapeDtypeStruct((n, D), F32),
                   jax.ShapeDtypeStruct((N_EXPERTS, n), F32)],
        grid=(n // tm,),
        in_specs=[pl.BlockSpec((tm, D), row), _mod_spec(mod_all, msel, l, tpb), _layer_spec(W["norm1"], l),
                  _layer_spec(W["norm2"], l)] + [pl.BlockSpec((tm, W_ATTN_OUT), row)] * 3
                 + [_layer_spec(w, l) for w in ws],
        out_specs=[pl.BlockSpec((tm, D), row), pl.BlockSpec((tm, D), row),
                   pl.BlockSpec((N_EXPERTS, tm), lambda i: (0, i))],
        compiler_params=_cparams(("parallel",)), name="out_proj",
    )(x, mod_all, W["norm1"], W["norm2"], om, og, od, *ws)


def _route_body(cap, aff_ref, idx_ref, gate_ref, blk_ref, rot_ref):
    v = aff_ref[...]
    E, n = v.shape
    nbits = max(1, (n - 1).bit_length())
    lane = _lane_iota((E, n))

    def count(mask):
        return jnp.sum(mask.astype(I32), axis=1, keepdims=True)

    def step(i, t):
        cand = t | lax.shift_left(jnp.int32(1), 30 - i)
        return jnp.where(count(v >= lax.bitcast_convert_type(cand, F32)) >= cap, cand, t)

    thr = lax.bitcast_convert_type(lax.fori_loop(0, 31, step, jnp.zeros((E, 1), I32)), F32)

    def prefix(u):
        for b in range(nbits):
            s = 1 << b
            u = u + jnp.where(lane >= s, pltpu.roll(u, s, 1), 0)
        return u

    gt = v > thr
    eq = v == thr
    need = cap - count(gt)
    eq_i = eq.astype(I32)
    eq_rank = prefix(eq_i) - eq_i
    sel = jnp.where(gt | (eq & (eq_rank < need)), 1, 0).astype(I32)
    dist = lane + 1 - prefix(sel)
    tok = lane
    for b in range(nbits):
        s = 1 << b
        mv = sel * ((dist >> b) & 1)
        stay = sel - mv
        take = pltpu.roll(mv, n - s, 1) == 1
        tok = jnp.where(take, pltpu.roll(tok, n - s, 1), tok)
        v = jnp.where(take, pltpu.roll(v, n - s, 1), v)
        dist = jnp.where(take, pltpu.roll(dist, n - s, 1), dist)
        sel = jnp.where(take, 1, stay)
    tok = tok[:, :cap]
    idx_ref[...] = tok
    gate_ref[...] = v[:, :cap]
    blk_ref[...] = tok >> 3
    rot_ref[...] = (_lane_iota((E, cap)) - tok) & (SUBLANES - 1)


def _route(aff_t, cap):
    E, n = aff_t.shape
    return pl.pallas_call(
        functools.partial(_route_body, cap),
        out_shape=[jax.ShapeDtypeStruct((E, cap), I32), jax.ShapeDtypeStruct((E, cap), F32),
                   jax.ShapeDtypeStruct((E, cap), I32), jax.ShapeDtypeStruct((E, cap), I32)],
        compiler_params=pltpu.CompilerParams(vmem_limit_bytes=VMEM_LIMIT), name="route",
    )(aff_t)


def _gather_body(tc, blk_ref, rot_ref, xs_ref, o_ref):
    i = pl.program_id(0)
    base = i * tc

    group = 16
    D = xs_ref.shape[2]
    sub = lax.broadcasted_iota(I32, (SUBLANES, D), 0)

    def body(g, _):
        halves = []
        for hf in range(group // SUBLANES):
            acc = jnp.zeros((SUBLANES, D), F32)
            for j in range(SUBLANES):
                s = base + g * group + hf * SUBLANES + j
                blk8 = pltpu.roll(xs_ref[blk_ref[s]], rot_ref[s], 0)
                acc = jnp.where(sub == j, blk8, acc)
            halves.append(acc)
        rows = pl.ds(pl.multiple_of(g * group, group), group)
        o_ref[rows, :] = jnp.concatenate(halves, axis=0).astype(BF16)
        return 0

    lax.fori_loop(0, tc // group, body, 0)


def _gather(blk, rot, xs, tc):
    E, cap = blk.shape
    n, D = xs.shape
    tiles = cap // tc
    return pl.pallas_call(
        functools.partial(_gather_body, tc),
        out_shape=jax.ShapeDtypeStruct((E * cap, D), BF16),
        grid_spec=pltpu.PrefetchScalarGridSpec(
            num_scalar_prefetch=2, grid=(E * tiles,),
            in_specs=[pl.BlockSpec((n // SUBLANES, SUBLANES, D), lambda i, b, r: (0, 0, 0),
                                   pipeline_mode=pl.Buffered(1))],
            out_specs=pl.BlockSpec((tc, D), lambda i, b, r: (i, 0))),
        compiler_params=_cparams(("arbitrary",)), name="moe_gather",
    )(blk.reshape(-1), rot.reshape(-1), xs.reshape(n // SUBLANES, SUBLANES, D))


def _ffn_body(ngroups, *refs):
    xe_refs = refs[:ngroups]
    wg_ref, wu_ref, wd_ref = refs[ngroups:ngroups + 3]
    o_refs = refs[ngroups + 3:]
    wg, wu, wd = wg_ref[0, 0].astype(BF16), wu_ref[0, 0].astype(BF16), wd_ref[0, 0].astype(BF16)
    for xe_ref, o_ref in zip(xe_refs, o_refs):
        xe = xe_ref[...]
        hg = _dot(xe, wg)
        hu = _dot(xe, wu)
        a = (hg / (1.0 + jnp.exp(-hg))) * hu
        o_ref[...] = _dot(a.astype(BF16), wd)


def _ffn(xes, wg, wu, wd, l, tc):
    _, E, D, FF = wg.shape
    rows = xes[0].shape[0]
    assert all(x.shape[0] == rows for x in xes)
    tiles = rows // E // tc
    wmap = lambda e, t: (l, e, 0, 0)
    row = lambda e, t: (e * tiles + t, 0)
    return pl.pallas_call(
        functools.partial(_ffn_body, len(xes)),
        out_shape=[jax.ShapeDtypeStruct((rows, D), F32)] * len(xes),
        grid=(E, tiles),
        in_specs=[pl.BlockSpec((tc, D), row)] * len(xes) + [pl.BlockSpec((1, 1, D, FF), wmap),
                  pl.BlockSpec((1, 1, D, FF), wmap), pl.BlockSpec((1, 1, FF, D), wmap)],
        out_specs=[pl.BlockSpec((tc, D), row)] * len(xes),
        compiler_params=_cparams(("parallel", "parallel")), name="moe_ffn",
    )(*xes, wg, wu, wd)


def _combine_body(tc, n_scatter, final, idx_ref, gate_ref, ye_ref, x1_ref, mod_ref, fn_ref,
                  o_ref, acc_ref):
    i = pl.program_id(0)

    @pl.when(i == 0)
    def _():
        acc_ref[...] = jnp.zeros_like(acc_ref)

    @pl.when(i < n_scatter)
    def _():
        base = i * tc

        sub = lax.broadcasted_iota(I32, (SUBLANES, LANES), 0)

        nlane = ye_ref.shape[1] // LANES
        ways = 4

        def body(gq, _):
            s0 = pl.multiple_of(gq * (ways * SUBLANES), ways * SUBLANES)
            ye8 = [ye_ref[pl.ds(s0 + w * SUBLANES, SUBLANES), :] for w in range(ways)]
            for j in range(SUBLANES):
                upd = []
                for w in range(ways):
                    r = idx_ref[base + s0 + w * SUBLANES + j]
                    g = gate_ref[base + s0 + w * SUBLANES + j]
                    r8 = pl.multiple_of((r >> 3) << 3, SUBLANES)
                    rolled = pltpu.roll(ye8[w], (r - j) & (SUBLANES - 1), 0)
                    gmask = jnp.where(sub == (r & (SUBLANES - 1)), g, 0.0)
                    upd.append((r8, acc_ref[pl.ds(r8, SUBLANES), :]
                                + rolled * jnp.concatenate([gmask] * nlane, axis=1)))
                for r8, val in upd:
                    acc_ref[pl.ds(r8, SUBLANES), :] = val
            return 0

        lax.fori_loop(0, tc // (ways * SUBLANES), body, 0)

    @pl.when(i >= n_scatter)
    def _():
        tm = x1_ref.shape[0]
        j = i - n_scatter
        rows = pl.ds(pl.multiple_of(j * tm, tm), tm)
        xn = x1_ref[...] + mod_ref[0, 5:6, :] * acc_ref[rows, :]
        o_ref[...] = _rms(xn, fn_ref[...]) if final else xn


def _combine(idx, gate, ye, x1, mod_all, msel, l, final_norm, tc, T, final):
    E, cap = idx.shape
    n, D = x1.shape
    tm = _row_tile(n, T, msel)
    tiles = cap // tc
    n_sc = E * tiles
    tpb = max(T // tm, 1)
    out_tile = lambda i: jnp.maximum(i - n_sc, 0)
    out_row = lambda i, a, b: (out_tile(i), 0)
    return pl.pallas_call(
        functools.partial(_combine_body, tc, n_sc, final),
        out_shape=jax.ShapeDtypeStruct((n, D), F32),
        grid_spec=pltpu.PrefetchScalarGridSpec(
            num_scalar_prefetch=2, grid=(n_sc + n // tm,),
            in_specs=[pl.BlockSpec((tc, D), lambda i, a, b: (jnp.minimum(i, n_sc - 1), 0)),
                      pl.BlockSpec((tm, D), out_row), _mod_spec(mod_all, msel, l, tpb, out_tile),
                      pl.BlockSpec((1, D), lambda i, a, b: (0, 0))],
            out_specs=pl.BlockSpec((tm, D), out_row),
            scratch_shapes=[pltpu.VMEM((n, D), F32)]),
        compiler_params=_cparams(("arbitrary",)), name="moe_combine",
    )(idx.reshape(-1), gate.reshape(-1), ye, x1, mod_all, final_norm)


def _rope_tables(T):
    t = jnp.arange(T)
    row = (t // GRID_W).astype(F32)
    col = (t % GRID_W).astype(F32)

    def table(head_w, lane_off, rep):
        half = head_w // 4
        freqs = ROPE_BASE ** (-jnp.arange(half, dtype=F32) / half)
        blk = []
        for pos in (row, col):
            ang = pos[:, None] * freqs[None, :]
            blk.append((jnp.cos(ang), jnp.sin(ang)))
        cos = jnp.concatenate([blk[0][0], blk[0][0], blk[1][0], blk[1][0]], axis=1)
        sin = jnp.concatenate([-blk[0][1], blk[0][1], -blk[1][1], blk[1][1]], axis=1)
        cos = jnp.concatenate([cos] * rep, axis=1)
        sin = jnp.concatenate([sin] * rep, axis=1)
        pad_r = LANES - lane_off - cos.shape[1]
        cos = jnp.pad(cos, ((0, 0), (lane_off, pad_r)), constant_values=1.0)
        sin = jnp.pad(sin, ((0, 0), (lane_off, pad_r)))
        return cos, sin

    c64, s64 = table(GQA_DIM, 0, 2)
    cm, sm = table(MLA_ROPE, MLA_NOPE, 1)
    return c64, s64, cm, sm


def _prep_weights(w_in, norm1, norm2, mla_q_norm, mla_kv_norm, mla_w_uq, mla_w_uk, mla_w_uv,
                  gqa_q_norm, gqa_k_norm, w_br_mla, w_br_gqa, w_br_diff, w_out, w_router):
    L, D = w_in.shape[0], D_MODEL
    w1 = jnp.concatenate([w_in[:, :, :_IN_KPE], w_in[:, :, _IN_GQ:_IN_GATES], w_in[:, :, _IN_KPE:_IN_GQ],
                          jnp.zeros((L, D, _W1_COLS - _O_KPE - MLA_ROPE), F32)], axis=2).astype(BF16)
    wgate = w_in[:, :, _IN_GATES:].astype(BF16)
    uq = mla_w_uq.reshape(L, MLA_Q_RANK, MLA_HEADS, MLA_NOPE + MLA_ROPE)
    wuq = jnp.pad(uq, ((0, 0), (0, 0), (0, 0), (0, HEAD_PAD - MLA_NOPE - MLA_ROPE))).reshape(L, MLA_Q_RANK, -1)
    wuk = jnp.pad(mla_w_uk, ((0, 0), (0, 0), (0, 0), (0, HEAD_PAD - MLA_NOPE))).reshape(L, MLA_KV_RANK, -1)
    wuv = mla_w_uv.reshape(L, MLA_KV_RANK, MLA_HEADS * MLA_V)
    grp = jnp.arange(W_Q_GQA) // GQA_DIM
    bd = (grp[:, None] == grp[None, :]).astype(BF16)
    wr = jnp.pad(w_router, ((0, 0), (0, 0), (0, LANES - N_EXPERTS)))
    wr_hi = wr.astype(BF16)
    wr_lo = (wr - wr_hi.astype(F32)).astype(BF16)
    vec = lambda a: a.reshape(L, 1, -1)
    return {
        "norm1": vec(norm1), "norm2": vec(norm2), "w1": w1,
        "mla_q_norm": vec(mla_q_norm), "mla_kv_norm": vec(mla_kv_norm),
        "gqa_q_norm": vec(jnp.tile(gqa_q_norm, (1, GQA_HEADS))),
        "gqa_k_norm": vec(jnp.tile(gqa_k_norm, (1, GQA_KV_HEADS))),
        "wuq": wuq.astype(BF16), "wuk": wuk.astype(BF16), "wuv": wuv.astype(BF16), "bd": bd,
        "wgate": wgate, "wbr_mla": w_br_mla.astype(BF16),
        "wbr_gqa": w_br_gqa.astype(BF16), "wbr_diff": w_br_diff.astype(BF16),
        "w_out": w_out.astype(BF16), "wr_hi": wr_hi, "wr_lo": wr_lo,
    }


def _mixers_and_routing(x, mod_all, msel, W, l, lam, lam_init, tabs, ctx, B, T, own):
    n = B * T
    outs = _in_proj(x, mod_all, msel, W, l, tabs, T, own)
    qm, km, vm, gq, gk, gv, dq, dk, dv = [a.reshape(B, T, -1) for a in outs[:9]]
    if ctx is None:
        pm, pg, pd = [(km, vm)], [(gk, gv)], [(dk, dv)]
    else:
        ckm, cvm, cgk, cgv, cdk, cdv = ctx
        pm, pg, pd = [(ckm, cvm), (km, vm)], [(cgk, cgv), (gk, gv)], [(cdk, cdv), (dk, dv)]
    om = _attention("mla", qm, pm, []).reshape(n, -1)
    og = _attention("gqa", gq, pg, []).reshape(n, -1)
    od = _attention("diff", dq, pd, list(lam), l, lam_init).reshape(n, -1)
    x1, h2, aff = _out_proj(x, mod_all, msel, om, og, od, W, l, T)
    cap = CAPACITY_FACTOR * n // N_EXPERTS
    idx, gate, blk, rot = _route(aff, cap)
    tc = min(MOE_ROW_TILE, cap)
    xe = _gather(blk, rot, h2, tc)
    return (idx, gate, xe, x1, tc), outs[9:]


def _experts_and_combine(routed, ew, l, mod_all, msels, Ts, final_norm, final):
    caps = {r[2].shape[0] for r in routed}
    if len(caps) == 1:
        cap = routed[0][0].shape[1]
        yes = _ffn([r[2] for r in routed], ew[0], ew[1], ew[2], l, min(FFN_TILE // len(routed), cap))
    else:
        yes = [_ffn([r[2]], ew[0], ew[1], ew[2], l, min(FFN_TILE, r[0].shape[1]))[0] for r in routed]
    return [_combine(idx, gate, ye, x1, mod_all, msel, l, final_norm, tc, T, final)
            for (idx, gate, _, x1, tc), ye, msel, T in zip(routed, yes, msels, Ts)]


def kernel(x_prompt, x_sample, cache_mla_ckv, cache_mla_kpe, cache_gqa_k, cache_gqa_v, cache_diff_k, cache_diff_v, c, c_ctx, w_ada, b_ada, norm1, norm2, w_in, mla_q_norm, mla_kv_norm, mla_w_uq, mla_w_uk, mla_w_uv, gqa_q_norm, gqa_k_norm, diff_lam_q1, diff_lam_k1, diff_lam_q2, diff_lam_k2, diff_subln, w_br_mla, w_br_gqa, w_br_diff, w_out, w_router, w_gate, w_up, w_down, final_norm):
    Bp, Tp, D = x_prompt.shape
    Bs, Ts, _ = x_sample.shape
    L = w_in.shape[0]
    P = cache_mla_ckv.shape[2]
    assert D == D_MODEL and Ts % GRID_W == 0

    nrow = -(-(1 + Bs) // 8) * 8
    cond = jnp.zeros((nrow, D), F32).at[0].set(c_ctx).at[1:1 + Bs].set(c)
    mod_all = _ada(cond, w_ada, b_ada).reshape(L, nrow, 6, D)

    tabs = _rope_tables(Ts)
    fn = final_norm[None]
    xp = x_prompt.reshape(Bp * Tp, D)
    xs = x_sample.reshape(Bs * Ts, D)
    own = ()
    W = _prep_weights(w_in, norm1, norm2, mla_q_norm, mla_kv_norm, mla_w_uq, mla_w_uk, mla_w_uv,
                      gqa_q_norm, gqa_k_norm, w_br_mla, w_br_gqa, w_br_diff, w_out, w_router)
    lam = tuple(a.reshape(L, 1, -1) for a in (diff_lam_q1, diff_lam_k1, diff_lam_q2, diff_lam_k2, diff_subln))
    kpe_pad_all = jnp.pad(cache_mla_kpe, ((0, 0), (0, 0), (0, 0), (0, LANES - MLA_ROPE)))
    msels = ((0, False), (1, True))
    for l in range(L):
        lam_init = 0.8 - 0.6 * math.exp(-0.3 * l)
        routed_p, own = _mixers_and_routing(xp, mod_all, msels[0], W, l, lam, lam_init, None, None, Bp, Tp, own)
        kpe_pad = kpe_pad_all[:, l].reshape(Bs * P, LANES)
        ckm, cvm = _cache_proj(cache_mla_ckv[:, l].reshape(Bs * P, MLA_KV_RANK), kpe_pad, W, l)
        ctx = (ckm.reshape(Bs, P, -1), cvm.reshape(Bs, P, -1),
               cache_gqa_k[:, l].reshape(Bs, P, -1), cache_gqa_v[:, l].reshape(Bs, P, -1),
               cache_diff_k, cache_diff_v)
        routed_s, _ = _mixers_and_routing(xs, mod_all, msels[1], W, l, lam, lam_init, tabs, ctx, Bs, Ts, None)
        xp, xs = _experts_and_combine([routed_p, routed_s], (w_gate, w_up, w_down), l, mod_all, msels,
                                      (Tp, Ts), fn, l == L - 1)

    return (xp.reshape(Bp, Tp, D), xs.reshape(Bs, Ts, D)) + tuple(own)
```
